```python
import jax, jax.numpy as jnp
from jax import lax
import numpy as np

D_MODEL = 1024
BATCH = 32
SEQ = 256
DEPTH = 2
DEC_BATCH = 4
DEC_SEQ = 2048
PAST_LEN = 256

GRID_W = 64
N_HEADS = 16
N_KV_HEADS = 4
HEAD_DIM = D_MODEL // N_HEADS
GQA_GROUP = N_HEADS // N_KV_HEADS
QKV_COLS = (N_HEADS + 2 * N_KV_HEADS) * HEAD_DIM
WINDOW = 128
BLOCK = 128
ROPE_THETA = 10000.0
CONV_WIDTH = 31
D_FF = 2816
N_EXPERTS = 8
TOP_K = 2
D_FF_EXPERT = 3584
N_MOD = 6
EPS = 1e-6
NEG_INF = -1e30

kernel_name = 'hybrid_diffusion_conv_swa_moe_step'


def _rmsnorm(x, g):
    xf = x.astype(jnp.float32)
    y = xf * lax.rsqrt(jnp.mean(xf * xf, axis=-1, keepdims=True) + EPS)
    return (y * g.astype(jnp.float32)).astype(x.dtype)


def _adaln(cond, w, b):
    m = jax.nn.silu(cond) @ w + b
    return m.reshape(cond.shape[0], N_MOD, D_MODEL)[:, :, None, :]


def _modulate(h, shift, scale):
    return h * (1 + scale) + shift


def _swiglu(h, w13, w2):
    u, g = jnp.split(h @ w13, 2, axis=-1)
    return (jax.nn.silu(u) * g) @ w2


def _conv_module(h, w1, b1, dw, dwb, ln_g, ln_b, w2):
    u, v = jnp.split(h @ w1 + b1, 2, axis=-1)
    g = u * jax.nn.sigmoid(v)
    pad = CONV_WIDTH // 2
    y = lax.conv_general_dilated(g, dw[:, None, :].astype(g.dtype), window_strides=(1,),
                                 padding=[(pad, pad)], dimension_numbers=('NWC', 'WIO', 'NWC'),
                                 feature_group_count=D_MODEL) + dwb
    yf = y.astype(jnp.float32)
    mu = jnp.mean(yf, axis=-1, keepdims=True)
    var = jnp.mean(jnp.square(yf - mu), axis=-1, keepdims=True)
    yn = ((yf - mu) * lax.rsqrt(var + EPS) * ln_g.astype(jnp.float32) + ln_b.astype(jnp.float32)).astype(h.dtype)
    return jax.nn.silu(yn) @ w2


def _qkv(h, wqkv):
    B, T, _ = h.shape
    y = h @ wqkv
    nq = N_HEADS * HEAD_DIM
    nk = N_KV_HEADS * HEAD_DIM
    q = y[..., :nq].reshape(B, T, N_HEADS, HEAD_DIM)
    k = y[..., nq:nq + nk].reshape(B, T, N_KV_HEADS, HEAD_DIM)
    v = y[..., nq + nk:].reshape(B, T, N_KV_HEADS, HEAD_DIM)
    return q, k, v


def _rope_2d(x):
    T = x.shape[1]
    rows = T // GRID_W
    row = jnp.repeat(jnp.arange(rows), GRID_W)
    col = jnp.tile(jnp.arange(GRID_W), rows)
    half = HEAD_DIM // 2
    inv = ROPE_THETA ** (-jnp.arange(0, half, 2, dtype=jnp.float32) / half)

    def rot(xa, pos):
        ang = pos.astype(jnp.float32)[:, None] * inv[None, :]
        cos = jnp.cos(ang)[None, :, None, :]
        sin = jnp.sin(ang)[None, :, None, :]
        xf = xa.astype(jnp.float32)
        x1, x2 = jnp.split(xf, 2, axis=-1)
        return jnp.concatenate([x1 * cos - x2 * sin, x2 * cos + x1 * sin], axis=-1)

    return jnp.concatenate([rot(x[..., :half], row), rot(x[..., half:], col)], axis=-1).astype(x.dtype)


def _sink_col(sink, lead):
    sk = sink.astype(jnp.float32).reshape(N_KV_HEADS, GQA_GROUP)[None, :, :, None, None]
    return jnp.broadcast_to(sk, lead + (1,))


def _ctx_attention(q, k, v, sink):
    B, S = q.shape[:2]
    nq = S // BLOCK
    qb = jnp.moveaxis(q.reshape(B, nq, BLOCK, N_KV_HEADS, GQA_GROUP, HEAD_DIM), 1, 0)
    scale = HEAD_DIM ** -0.5

    def one(qi):
        s = jnp.einsum('bqkgd,bskd->bkgqs', qi, k, preferred_element_type=jnp.float32) * scale
        p = jax.nn.softmax(jnp.concatenate([s, _sink_col(sink, s.shape[:-1])], axis=-1), axis=-1)
        return jnp.einsum('bkgqs,bskd->bqkgd', p[..., :-1].astype(v.dtype), v)

    o = lax.map(one, qb)
    return jnp.moveaxis(o, 0, 1).reshape(B, S, D_MODEL)


def _latent_attention(q, k, v, ck, cv, sink):
    B, T = q.shape[:2]
    nb = T // BLOCK
    L = 3 * BLOCK
    P = ck.shape[1]
    scale = HEAD_DIM ** -0.5

    def band(a):
        ap = jnp.pad(a, ((0, 0), (BLOCK, BLOCK), (0, 0), (0, 0))).reshape(B, nb + 2, BLOCK, N_KV_HEADS, HEAD_DIM)
        return jnp.moveaxis(jnp.concatenate([ap[:, :-2], ap[:, 1:-1], ap[:, 2:]], axis=2), 1, 0)

    qb = jnp.moveaxis(q.reshape(B, nb, BLOCK, N_KV_HEADS, GQA_GROUP, HEAD_DIM), 1, 0)
    kb, vb = band(k), band(v)
    bi = jnp.arange(nb)[:, None, None]
    qpos = bi * BLOCK + jnp.arange(BLOCK)[None, :, None]
    kpos = (bi - 1) * BLOCK + jnp.arange(L)[None, None, :]
    valid = (jnp.abs(qpos - kpos) <= WINDOW) & (kpos >= 0) & (kpos < T)

    def one(args):
        qi, ki, vi, mi = args
        s_loc = jnp.einsum('bqkgd,bskd->bkgqs', qi, ki, preferred_element_type=jnp.float32) * scale
        s_loc = jnp.where(mi[None, None, None], s_loc, NEG_INF)
        s_ctx = jnp.einsum('bqkgd,bpkd->bkgqp', qi, ck, preferred_element_type=jnp.float32) * scale
        s = jnp.concatenate([s_loc, s_ctx, _sink_col(sink, s_loc.shape[:-1])], axis=-1)
        p = jax.nn.softmax(s, axis=-1).astype(vi.dtype)
        return (jnp.einsum('bkgqs,bskd->bqkgd', p[..., :L], vi)
                + jnp.einsum('bkgqp,bpkd->bqkgd', p[..., L:L + P], cv))

    o = lax.map(one, (qb, kb, vb, valid))
    return jnp.moveaxis(o, 0, 1).reshape(B, T, D_MODEL)


def _moe(h, router, w13, w2):
    B, T, D = h.shape
    xt = h.reshape(-1, D)
    logits = jnp.dot(xt, router, preferred_element_type=jnp.float32)
    topv, topi = lax.top_k(logits, TOP_K)
    gates = jax.nn.softmax(topv, axis=-1)
    combine = jnp.sum(jax.nn.one_hot(topi, N_EXPERTS, dtype=jnp.float32) * gates[..., None], axis=1)
    y = jnp.zeros_like(xt)
    for e in range(N_EXPERTS):
        y = y + combine[:, e:e + 1].astype(xt.dtype) * _swiglu(xt, w13[e], w2[e])
    return y.reshape(B, T, D)


def setup_inputs(seed: int = 0) -> dict:
    key = jax.random.key(seed)
    ks = jax.random.split(key, 25)
    f32 = jnp.float32

    def nrm(k, shape, scale):
        return jax.random.normal(k, shape, f32) * scale

    D = D_MODEL
    return {
        'x_prompt': nrm(ks[0], (BATCH, SEQ, D), 1.0),
        'x_sample': nrm(ks[1], (DEC_BATCH, DEC_SEQ, D), 1.0),
        'cache_k_l1': nrm(ks[2], (DEC_BATCH, PAST_LEN, N_KV_HEADS, HEAD_DIM), 1.0),
        'cache_v_l1': nrm(ks[3], (DEC_BATCH, PAST_LEN, N_KV_HEADS, HEAD_DIM), 1.0),
        'c': nrm(ks[4], (DEC_BATCH, D), 1.0),
        'c_ctx': nrm(ks[5], (D,), 1.0),
        'ada_w': nrm(ks[6], (DEPTH, D, N_MOD * D), 0.5 * D ** -0.5),
        'ada_b': nrm(ks[7], (DEPTH, N_MOD * D), 0.02),
        'norm_g': 1.0 + nrm(ks[8], (DEPTH, 2, D), 0.1),
        'conv_w1': nrm(ks[9], (D, 2 * D), D ** -0.5),
        'conv_b1': nrm(ks[10], (2 * D,), 0.02),
        'conv_dw': nrm(ks[11], (CONV_WIDTH, D), CONV_WIDTH ** -0.5),
        'conv_dwb': nrm(ks[12], (D,), 0.02),
        'conv_ln_g': 1.0 + nrm(ks[13], (D,), 0.1),
        'conv_ln_b': nrm(ks[14], (D,), 0.02),
        'conv_w2': nrm(ks[15], (D, D), D ** -0.5),
        'attn_wqkv': nrm(ks[16], (D, QKV_COLS), D ** -0.5),
        'attn_wo': nrm(ks[17], (D, D), D ** -0.5),
        'attn_sink': nrm(ks[18], (N_HEADS,), 1.0),
        'ffn_w13': nrm(ks[19], (D, 2 * D_FF), D ** -0.5),
        'ffn_w2': nrm(ks[20], (D_FF, D), D_FF ** -0.5),
        'moe_router': nrm(ks[21], (D, N_EXPERTS), D ** -0.5),
        'moe_w13': nrm(ks[22], (N_EXPERTS, D, 2 * D_FF_EXPERT), D ** -0.5),
        'moe_w2': nrm(ks[23], (N_EXPERTS, D_FF_EXPERT, D), D_FF_EXPERT ** -0.5),
        'final_g': 1.0 + nrm(ks[24], (D,), 0.1),
    }


def reference(x_prompt, x_sample, cache_k_l1, cache_v_l1, c, c_ctx, ada_w, ada_b, norm_g,
              conv_w1, conv_b1, conv_dw, conv_dwb, conv_ln_g, conv_ln_b, conv_w2,
              attn_wqkv, attn_wo, attn_sink, ffn_w13, ffn_w2, moe_router, moe_w13, moe_w2, final_g):
    n_p = x_prompt.shape[0]
    cond_p = jnp.broadcast_to(c_ctx[None, :], (n_p, D_MODEL))
    xp, xs = x_prompt, x_sample
    state_k_l1 = None
    state_v_l1 = None
    for i in range(DEPTH):
        mp = _adaln(cond_p, ada_w[i], ada_b[i])
        ms = _adaln(c, ada_w[i], ada_b[i])
        hp = _modulate(_rmsnorm(xp, norm_g[i, 0]), mp[:, 0], mp[:, 1])
        hs = _modulate(_rmsnorm(xs, norm_g[i, 0]), ms[:, 0], ms[:, 1])
        if i % 2 == 0:
            op = _conv_module(hp, conv_w1, conv_b1, conv_dw, conv_dwb, conv_ln_g, conv_ln_b, conv_w2)
            os_ = _conv_module(hs, conv_w1, conv_b1, conv_dw, conv_dwb, conv_ln_g, conv_ln_b, conv_w2)
        else:
            qp, kp, vp = _qkv(hp, attn_wqkv)
            op = _ctx_attention(qp, kp, vp, attn_sink) @ attn_wo
            state_k_l1, state_v_l1 = kp, vp
            qs, ks_, vs = _qkv(hs, attn_wqkv)
            os_ = _latent_attention(_rope_2d(qs), _rope_2d(ks_), vs, cache_k_l1, cache_v_l1, attn_sink) @ attn_wo
        xp = xp + mp[:, 2] * op
        xs = xs + ms[:, 2] * os_
        hp = _modulate(_rmsnorm(xp, norm_g[i, 1]), mp[:, 3], mp[:, 4])
        hs = _modulate(_rmsnorm(xs, norm_g[i, 1]), ms[:, 3], ms[:, 4])
        if i % 2 == 0:
            fp = _swiglu(hp, ffn_w13, ffn_w2)
            fs = _swiglu(hs, ffn_w13, ffn_w2)
        else:
            fp = _moe(hp, moe_router, moe_w13, moe_w2)
            fs = _moe(hs, moe_router, moe_w13, moe_w2)
        xp = xp + mp[:, 5] * fp
        xs = xs + ms[:, 5] * fs
    y_prompt = _rmsnorm(xp, final_g)
    y_sample = _rmsnorm(xs, final_g)
    return (y_prompt, y_sample, state_k_l1, state_v_l1)
```

```python
import functools

import jax
import jax.numpy as jnp
from jax import lax
from jax.experimental import pallas as pl
from jax.experimental.pallas import tpu as pltpu

F32 = jnp.float32
BF16 = jnp.bfloat16

D = 1024
BATCH, SEQ = 32, 256
DEC_BATCH, DEC_SEQ = 4, 2048
PAST = 256
NP_TOK = BATCH * SEQ
NS_TOK = DEC_BATCH * DEC_SEQ
NTOK = NP_TOK + NS_TOK
GRID_W = 64
N_HEADS, N_KV, HD = 16, 4, 64
GQA = N_HEADS // N_KV
KV_COLS = N_KV * HD
QKV_COLS = D + 2 * KV_COLS
WINDOW = 128
ROPE_THETA = 10000.0
CONV_W = 31
CONV_PAD = CONV_W // 2
D_FF = 2816
N_EXP = 8
D_FFE = 3584
N_MOD = 6
EPS = 1e-6
NEG_INF = -1e30

N_COND = 8
CTX_ROW = DEC_BATCH

CONV_T = 256
HALO = 16
ROW_T = 512
ATT_T = 256
ATT_WIN = ATT_T + 2 * WINDOW
FF_T = 1408
MOE_T = 512
MOE_FT = 1792
MOE_TILES = 2 * NTOK // MOE_T + N_EXP
MOE_ROWS = MOE_TILES * MOE_T
DMA_T = 1024
LANES = 128

VMEM_LIMIT = 56 * 1024 * 1024


def _cparams(n_axes=1, vmem=VMEM_LIMIT):
    return pltpu.CompilerParams(dimension_semantics=("arbitrary",) * n_axes,
                                vmem_limit_bytes=vmem)


def _silu(x):
    return x * jax.nn.sigmoid(x)


def _norm_mod(x, g, shift, scale):
    y = x * lax.rsqrt(jnp.mean(x * x, axis=-1, keepdims=True) + EPS)
    return (y * g) * (1.0 + scale) + shift


def _cond_row(tile, tile_rows):
    n_prompt_tiles = NP_TOK // tile_rows
    per_seq = DEC_SEQ // tile_rows
    return jnp.where(tile < n_prompt_tiles, CTX_ROW, (tile - n_prompt_tiles) // per_seq)


def _mod_spec(layer, tile_rows):
    return pl.BlockSpec((None, None, N_MOD, D),
                        lambda i, *_: (layer, _cond_row(i, tile_rows), 0, 0))


def _const_spec(shape):
    nd = len(shape)
    return pl.BlockSpec(shape, lambda *_: (0,) * nd)


ADA_TN = 1536


def _ada_kernel(cond_ref, w_ref, b_ref, o_ref):
    s = _silu(cond_ref[...]).astype(BF16)
    o_ref[...] = jnp.dot(s, w_ref[...].astype(BF16), preferred_element_type=F32) + b_ref[...]


def _ada(cond, ada_w, ada_b):
    depth = ada_w.shape[0]
    out = pl.pallas_call(
        _ada_kernel,
        out_shape=jax.ShapeDtypeStruct((depth, N_COND, N_MOD * D), F32),
        grid=(depth, N_MOD * D // ADA_TN),
        in_specs=[pl.BlockSpec((N_COND, D), lambda l, n: (0, 0)),
                  pl.BlockSpec((None, D, ADA_TN), lambda l, n: (l, 0, n)),
                  pl.BlockSpec((None, 1, ADA_TN), lambda l, n: (l, 0, n))],
        out_specs=pl.BlockSpec((None, N_COND, ADA_TN), lambda l, n: (l, 0, n)),
        compiler_params=_cparams(2),
        name="adaln",
    )(cond, ada_w, ada_b.reshape(depth, 1, N_MOD * D))
    return out.reshape(depth, N_COND, N_MOD, D)


CONV_PT = NP_TOK // CONV_T
CONV_TPS = DEC_SEQ // CONV_T
CONV_E = CONV_T + 2 * HALO


def _conv_kernel(xp_ref, xs_ref, xprev_ref, xnext_ref, mod_ref, ng_ref, w1_ref, b1_ref,
                 dw_ref, dwb_ref, lng_ref, lnb_ref, w2_ref, o_ref, g_scr, y_scr):
    i = pl.program_id(0)
    j = i - CONV_PT
    is_prompt = i < CONV_PT
    has_prev = jnp.logical_and(i >= CONV_PT, j % CONV_TPS != 0)
    has_next = jnp.logical_and(i >= CONV_PT, j % CONV_TPS != CONV_TPS - 1)
    x_main = jnp.where(is_prompt, xp_ref[...], xs_ref[...])
    shift, scale, gate = mod_ref[0:1], mod_ref[1:2], mod_ref[2:3]

    xe = jnp.concatenate([xprev_ref[...], x_main, xnext_ref[...]], axis=0)
    h = _norm_mod(xe, ng_ref[...], shift, scale).astype(BF16)
    uv = jnp.dot(h, w1_ref[...], preferred_element_type=F32) + b1_ref[...]
    g = uv[:, :D] * jax.nn.sigmoid(uv[:, D:])
    g_scr[0:HALO] = jnp.where(has_prev, g[0:HALO], 0.0)
    g_scr[HALO:HALO + CONV_T] = g[HALO:HALO + CONV_T]
    g_scr[HALO + CONV_T:] = jnp.where(has_next, g[HALO + CONV_T:], 0.0)

    base = HALO - CONV_PAD
    for c in range(D // LANES):
        cs = slice(c * LANES, (c + 1) * LANES)
        acc = jnp.zeros((CONV_T, LANES), F32) + dwb_ref[:, cs]
        for k in range(CONV_W):
            acc = acc + g_scr[base + k:base + k + CONV_T, cs] * dw_ref[k:k + 1, cs]
        y_scr[:, cs] = acc

    y = y_scr[...]
    mu = jnp.mean(y, axis=-1, keepdims=True)
    d = y - mu
    var = jnp.mean(d * d, axis=-1, keepdims=True)
    yn = d * lax.rsqrt(var + EPS) * lng_ref[...] + lnb_ref[...]
    o = jnp.dot(_silu(yn).astype(BF16), w2_ref[...], preferred_element_type=F32)
    o_ref[...] = x_main + gate * o


def _conv_layer(xp, xs, mods, ng, w1, b1, dw, dwb, lng, lnb, w2):
    n_tiles = NTOK // CONV_T
    hb = CONV_T // HALO
    n_hb = NS_TOK // HALO

    def s_tile(i):
        return jnp.maximum(i - CONV_PT, 0)

    return pl.pallas_call(
        _conv_kernel,
        out_shape=jax.ShapeDtypeStruct((NTOK, D), F32),
        grid=(n_tiles,),
        in_specs=[
            pl.BlockSpec((CONV_T, D), lambda i: (jnp.minimum(i, CONV_PT - 1), 0)),
            pl.BlockSpec((CONV_T, D), lambda i: (s_tile(i), 0)),
            pl.BlockSpec((HALO, D), lambda i: (jnp.maximum(s_tile(i) * hb - 1, 0), 0)),
            pl.BlockSpec((HALO, D), lambda i: (jnp.minimum(s_tile(i) * hb + hb, n_hb - 1), 0)),
            _mod_spec(0, CONV_T),
            _const_spec((1, D)),
            _const_spec((D, 2 * D)),
            _const_spec((1, 2 * D)),
            _const_spec((CONV_W, D)),
            _const_spec((1, D)),
            _const_spec((1, D)),
            _const_spec((1, D)),
            _const_spec((D, D)),
        ],
        out_specs=pl.BlockSpec((CONV_T, D), lambda i: (i, 0)),
        scratch_shapes=[pltpu.VMEM((CONV_E, D), F32), pltpu.VMEM((CONV_T, D), F32)],
        compiler_params=_cparams(1),
        name="conv_mixer",
    )(xp, xs, xs, xs, mods, ng, w1, b1, dw, dwb, lng, lnb, w2)


def _ffn_kernel(x_ref, mod_ref, ng_ref, w13_ref, w2_ref, o_ref):
    x = x_ref[...]
    h = _norm_mod(x, ng_ref[...], mod_ref[3:4], mod_ref[4:5]).astype(BF16)
    acc = jnp.zeros((ROW_T, D), F32)
    for f in range(D_FF // FF_T):
        u = jnp.dot(h, w13_ref[:, f * FF_T:(f + 1) * FF_T], preferred_element_type=F32)
        g = jnp.dot(h, w13_ref[:, D_FF + f * FF_T:D_FF + (f + 1) * FF_T],
                    preferred_element_type=F32)
        a = (_silu(u) * g).astype(BF16)
        acc = acc + jnp.dot(a, w2_ref[f * FF_T:(f + 1) * FF_T, :], preferred_element_type=F32)
    o_ref[...] = x + mod_ref[5:6] * acc


def _ffn_layer(x, mods, ng, w13, w2):
    return pl.pallas_call(
        _ffn_kernel,
        out_shape=jax.ShapeDtypeStruct((NTOK, D), F32),
        grid=(NTOK // ROW_T,),
        in_specs=[pl.BlockSpec((ROW_T, D), lambda i: (i, 0)),
                  _mod_spec(0, ROW_T),
                  _const_spec((1, D)),
                  pl.BlockSpec((D, 2 * D_FF), lambda i: (0, 0), pipeline_mode=pl.Buffered(1)),
                  pl.BlockSpec((D_FF, D), lambda i: (0, 0), pipeline_mode=pl.Buffered(1))],
        out_specs=pl.BlockSpec((ROW_T, D), lambda i: (i, 0)),
        compiler_params=_cparams(1),
        name="dense_ffn",
    )(x, mods, ng, w13, w2)


ROW_PT = NP_TOK // ROW_T
ROW_TPS = DEC_SEQ // ROW_T


def _rope_tables():
    half = HD // 2
    inv = ROPE_THETA ** (-jnp.arange(0, half, 2, dtype=F32) / half)
    t = jnp.arange(DEC_SEQ)
    lane = jnp.arange(LANES)
    jj = lane % HD
    pos = jnp.where(jj[None, :] < half, (t // GRID_W)[:, None], (t % GRID_W)[:, None]).astype(F32)
    ang = pos * inv[jj % (half // 2)][None, :]
    first = (jj % half) < (half // 2)
    cos = jnp.cos(ang)
    sin = jnp.where(first[None, :], -jnp.sin(ang), jnp.sin(ang))
    cos = jnp.concatenate([cos.reshape(ROW_TPS, ROW_T, LANES), jnp.ones((1, ROW_T, LANES), F32)])
    sin = jnp.concatenate([sin.reshape(ROW_TPS, ROW_T, LANES), jnp.zeros((1, ROW_T, LANES), F32)])
    return cos, sin


def _rope(x, cos, sin):
    lane = lax.broadcasted_iota(jnp.int32, x.shape, 1)
    first = (lane % (HD // 2)) < (HD // 4)
    swapped = jnp.where(first, pltpu.roll(x, LANES - HD // 4, axis=1), pltpu.roll(x, HD // 4, axis=1))
    return x * cos + swapped * sin


def _qkv_kernel(x_ref, mod_ref, ng_ref, w_ref, cos_ref, sin_ref,
                q_ref, k_ref, v_ref, sk_ref, sv_ref):
    i = pl.program_id(0)
    h = _norm_mod(x_ref[...], ng_ref[...], mod_ref[0:1], mod_ref[1:2]).astype(BF16)
    y = jnp.dot(h, w_ref[...], preferred_element_type=F32)
    k = y[:, D:D + KV_COLS]
    v = y[:, D + KV_COLS:]

    @pl.when(i < ROW_PT)
    def _():
        sk_ref[...] = k
        sv_ref[...] = v

    cos, sin = cos_ref[...], sin_ref[...]
    scale = HD ** -0.5
    for c in range(D // LANES):
        cs = slice(c * LANES, (c + 1) * LANES)
        q_ref[:, cs] = (_rope(y[:, cs], cos, sin) * scale).astype(BF16)
    for c in range(KV_COLS // LANES):
        cs = slice(c * LANES, (c + 1) * LANES)
        k_ref[:, cs] = _rope(k[:, cs], cos, sin).astype(BF16)
    v_ref[...] = v.astype(BF16)


def _qkv_layer(x, mods, ng, wqkv, cos, sin):
    def tab(i):
        return (jnp.where(i < ROW_PT, ROW_TPS, (i - ROW_PT) % ROW_TPS), 0, 0)

    def state(i):
        return (jnp.minimum(i, ROW_PT - 1), 0)

    return pl.pallas_call(
        _qkv_kernel,
        out_shape=(jax.ShapeDtypeStruct((NTOK, D), BF16),
                   jax.ShapeDtypeStruct((NTOK, KV_COLS), BF16),
                   jax.ShapeDtypeStruct((NTOK, KV_COLS), BF16),
                   jax.ShapeDtypeStruct((NP_TOK, KV_COLS), F32),
                   jax.ShapeDtypeStruct((NP_TOK, KV_COLS), F32)),
        grid=(NTOK // ROW_T,),
        in_specs=[pl.BlockSpec((ROW_T, D), lambda i: (i, 0)),
                  _mod_spec(1, ROW_T),
                  _const_spec((1, D)),
                  _const_spec((D, QKV_COLS)),
                  pl.BlockSpec((None, ROW_T, LANES), tab),
                  pl.BlockSpec((None, ROW_T, LANES), tab)],
        out_specs=(pl.BlockSpec((ROW_T, D), lambda i: (i, 0)),
                   pl.BlockSpec((ROW_T, KV_COLS), lambda i: (i, 0)),
                   pl.BlockSpec((ROW_T, KV_COLS), lambda i: (i, 0)),
                   pl.BlockSpec((ROW_T, KV_COLS), state),
                   pl.BlockSpec((ROW_T, KV_COLS), state)),
        compiler_params=_cparams(1),
        name="qkv_rope",
    )(x, mods, ng, wqkv, cos, sin)


def _attn_heads(q, parts, sink_ref, o_ref):
    nt = (((1,), (1,)), ((), ()))
    for h in range(N_HEADS):
        kv = h // GQA
        hs = slice(kv * HD, (kv + 1) * HD)
        qh = q[:, h * HD:(h + 1) * HD]
        sink = sink_ref[h]
        scores = []
        m = jnp.full((q.shape[0], 1), sink, F32)
        for k, _, mask in parts:
            s = lax.dot_general(qh, k[:, hs], nt, preferred_element_type=F32)
            if mask is not None:
                s = jnp.where(mask, s, NEG_INF)
            scores.append(s)
            m = jnp.maximum(m, jnp.max(s, axis=-1, keepdims=True))
        denom = jnp.exp(sink - m)
        o = jnp.zeros((q.shape[0], HD), F32)
        for s, (_, v, _) in zip(scores, parts):
            p = jnp.exp(s - m)
            denom = denom + jnp.sum(p, axis=-1, keepdims=True)
            o = o + jnp.dot(p.astype(BF16), v[:, hs], preferred_element_type=F32)
        o_ref[:, h * HD:(h + 1) * HD] = (o / denom).astype(o_ref.dtype)


def _ctx_attn_kernel(sink_ref, q_ref, k_ref, v_ref, o_ref):
    _attn_heads(q_ref[...], [(k_ref[...], v_ref[...], None)], sink_ref, o_ref)


def _lat_attn_kernel(sink_ref, q_ref, k_ref, v_ref, ck_ref, cv_ref, o_ref):
    j = pl.program_id(1)
    w0 = pl.multiple_of(jnp.clip(j * ATT_T - WINDOW, 0, DEC_SEQ - ATT_WIN), WINDOW)
    qpos = j * ATT_T + lax.broadcasted_iota(jnp.int32, (ATT_T, ATT_WIN), 0)
    kpos = w0 + lax.broadcasted_iota(jnp.int32, (ATT_T, ATT_WIN), 1)
    mask = jnp.abs(qpos - kpos) <= WINDOW
    parts = [(k_ref[pl.ds(w0, ATT_WIN), :], v_ref[pl.ds(w0, ATT_WIN), :], mask),
             (ck_ref[...].astype(BF16), cv_ref[...].astype(BF16), None)]
    _attn_heads(q_ref[...], parts, sink_ref, o_ref)


def _attention(q, k, v, ck, cv, sink):
    smem = pl.BlockSpec(memory_space=pltpu.SMEM)
    o_ctx = pl.pallas_call(
        _ctx_attn_kernel,
        out_shape=jax.ShapeDtypeStruct((NP_TOK, D), BF16),
        grid=(BATCH,),
        in_specs=[smem,
                  pl.BlockSpec((SEQ, D), lambda b: (b, 0)),
                  pl.BlockSpec((SEQ, KV_COLS), lambda b: (b, 0)),
                  pl.BlockSpec((SEQ, KV_COLS), lambda b: (b, 0))],
        out_specs=pl.BlockSpec((SEQ, D), lambda b: (b, 0)),
        compiler_params=_cparams(1),
        name="ctx_attention",
    )(sink, q, k, v)

    q_off = NP_TOK // ATT_T
    kv_off = NP_TOK // DEC_SEQ
    tps = DEC_SEQ // ATT_T
    o_lat = pl.pallas_call(
        _lat_attn_kernel,
        out_shape=jax.ShapeDtypeStruct((NS_TOK, D), BF16),
        grid=(DEC_BATCH, tps),
        in_specs=[smem,
                  pl.BlockSpec((ATT_T, D), lambda b, j: (q_off + b * tps + j, 0)),
                  pl.BlockSpec((DEC_SEQ, KV_COLS), lambda b, j: (kv_off + b, 0)),
                  pl.BlockSpec((DEC_SEQ, KV_COLS), lambda b, j: (kv_off + b, 0)),
                  pl.BlockSpec((None, PAST, KV_COLS), lambda b, j: (b, 0, 0)),
                  pl.BlockSpec((None, PAST, KV_COLS), lambda b, j: (b, 0, 0))],
        out_specs=pl.BlockSpec((ATT_T, D), lambda b, j: (b * tps + j, 0)),
        compiler_params=_cparams(2),
        name="latent_attention",
    )(sink, q, k, v, ck, cv)
    return o_ctx, o_lat


def _wo_router_kernel(oc_ref, ol_ref, x_ref, mod_ref, ng_ref, wo_ref, r_ref,
                      x3_ref, h_ref, meta_ref, cnt_ref, carry):
    i = pl.program_id(0)

    @pl.when(i == 0)
    def _():
        carry[...] = jnp.zeros_like(carry)

    a = jnp.where(i < ROW_PT, oc_ref[...], ol_ref[...])
    o = jnp.dot(a, wo_ref[...], preferred_element_type=F32)
    x3 = x_ref[...] + mod_ref[2:3] * o
    x3_ref[...] = x3
    h = _norm_mod(x3, ng_ref[...], mod_ref[3:4], mod_ref[4:5])
    h_ref[...] = h

    logits = jnp.dot(h, r_ref[...], preferred_element_type=F32, precision=lax.Precision.HIGHEST)
    lane = lax.broadcasted_iota(jnp.int32, (ROW_T, LANES), 1).astype(F32)
    logits = jnp.where(lane < N_EXP, logits, -jnp.inf)
    m1 = jnp.max(logits, axis=-1, keepdims=True)
    e1 = jnp.min(jnp.where(logits == m1, lane, float(LANES)), axis=-1, keepdims=True)
    rest = jnp.where(lane == e1, -jnp.inf, logits)
    m2 = jnp.max(rest, axis=-1, keepdims=True)
    e2 = jnp.min(jnp.where(rest == m2, lane, float(LANES)), axis=-1, keepdims=True)
    t = jnp.exp(m2 - m1)
    g1 = 1.0 / (1.0 + t)
    g2 = t / (1.0 + t)

    oh1 = (lane == e1).astype(F32)
    oh2 = (lane == e2).astype(F32)
    oh = oh1 + oh2
    r_i = lax.broadcasted_iota(jnp.int32, (ROW_T, ROW_T), 0)
    c_i = lax.broadcasted_iota(jnp.int32, (ROW_T, ROW_T), 1)
    tri = (c_i < r_i).astype(BF16)
    before = jnp.dot(tri, oh.astype(BF16), preferred_element_type=F32) + carry[0:1, :]
    rank1 = jnp.sum(before * oh1, axis=-1, keepdims=True)
    rank2 = jnp.sum(before * oh2, axis=-1, keepdims=True)
    carry[0:1, :] = carry[0:1, :] + jnp.sum(oh, axis=0, keepdims=True)
    cnt_ref[...] = carry[...]

    meta = jnp.where(lane == 0, e1, 0.0)
    meta = jnp.where(lane == 1, e2, meta)
    meta = jnp.where(lane == 2, rank1, meta)
    meta = jnp.where(lane == 3, rank2, meta)
    meta = jnp.where(lane == 4, g1, meta)
    meta = jnp.where(lane == 5, g2, meta)
    meta_ref[...] = meta


def _wo_router(o_ctx, o_lat, x, mods, ng, wo, router_pad):
    return pl.pallas_call(
        _wo_router_kernel,
        out_shape=(jax.ShapeDtypeStruct((NTOK, D), F32),
                   jax.ShapeDtypeStruct((NTOK, D), F32),
                   jax.ShapeDtypeStruct((NTOK, LANES), F32),
                   jax.ShapeDtypeStruct((8, LANES), F32)),
        grid=(NTOK // ROW_T,),
        in_specs=[pl.BlockSpec((ROW_T, D), lambda i: (jnp.minimum(i, ROW_PT - 1), 0)),
                  pl.BlockSpec((ROW_T, D), lambda i: (jnp.maximum(i - ROW_PT, 0), 0)),
                  pl.BlockSpec((ROW_T, D), lambda i: (i, 0)),
                  _mod_spec(1, ROW_T),
                  _const_spec((1, D)),
                  _const_spec((D, D)),
                  _const_spec((D, LANES))],
        out_specs=(pl.BlockSpec((ROW_T, D), lambda i: (i, 0)),
                   pl.BlockSpec((ROW_T, D), lambda i: (i, 0)),
                   pl.BlockSpec((ROW_T, LANES), lambda i: (i, 0)),
                   pl.BlockSpec((8, LANES), lambda i: (0, 0))),
        scratch_shapes=[pltpu.VMEM((8, LANES), F32)],
        compiler_params=_cparams(1),
        name="wo_router",
    )(o_ctx, o_lat, x, mods, ng, wo, router_pad)


def _row_copy(src, r, dst, p, sem):
    return pltpu.make_async_copy(src.at[pl.ds(r, 1)], dst.at[pl.ds(p, 1)], sem)


def _dispatch_kernel(tz_ref, pos1_ref, pos2_ref, h_ref, xs_ref, zero, sem, zsem):
    @pl.when(pl.program_id(0) == 0)
    def _():
        zero[...] = jnp.zeros_like(zero)

        def tile_copy(t):
            return pltpu.make_async_copy(zero, xs_ref.at[pl.ds(t * MOE_T, MOE_T)], zsem)

        def fill(t, c):
            @pl.when(tz_ref[t] == 1)
            def _():
                tile_copy(t).start()
            return c

        def drain(t, c):
            @pl.when(tz_ref[t] == 1)
            def _():
                tile_copy(t).wait()
            return c

        lax.fori_loop(0, MOE_TILES, fill, 0)
        lax.fori_loop(0, MOE_TILES, drain, 0)

    def issue(r, c):
        _row_copy(h_ref, r, xs_ref, pos1_ref[r], sem).start()
        _row_copy(h_ref, r, xs_ref, pos2_ref[r], sem).start()
        return c

    lax.fori_loop(0, DMA_T, issue, 0)
    for _ in range(2):
        pltpu.make_async_copy(h_ref, xs_ref.at[pl.ds(0, DMA_T)], sem).wait()


def _dispatch(tile_zero, pos1, pos2, h):
    smem_blk = pl.BlockSpec((DMA_T,), lambda i, *_: (i,), memory_space=pltpu.SMEM)
    return pl.pallas_call(
        _dispatch_kernel,
        out_shape=jax.ShapeDtypeStruct((MOE_ROWS, D), F32),
        grid_spec=pltpu.PrefetchScalarGridSpec(
            num_scalar_prefetch=1,
            grid=(NTOK // DMA_T,),
            in_specs=[smem_blk, smem_blk,
                      pl.BlockSpec((DMA_T, D), lambda i, *_: (i, 0))],
            out_specs=pl.BlockSpec(memory_space=pl.ANY),
            scratch_shapes=[pltpu.VMEM((MOE_T, D), F32), pltpu.SemaphoreType.DMA,
                            pltpu.SemaphoreType.DMA]),
        compiler_params=_cparams(1),
        name="moe_dispatch",
    )(tile_zero, pos1, pos2, h)


MOE_NF = D_FFE // MOE_FT


def _expert_kernel(te_ref, ts_ref, tv_ref, x_ref, wu_ref, wg_ref, w2_ref, o_ref, acc):
    j = pl.program_id(0)
    f = pl.program_id(1)

    @pl.when(tv_ref[j] == 1)
    def _():
        x = x_ref[...].astype(BF16)
        u = jnp.dot(x, wu_ref[...], preferred_element_type=F32)
        g = jnp.dot(x, wg_ref[...], preferred_element_type=F32)
        a = (_silu(u) * g).astype(BF16)
        y = jnp.dot(a, w2_ref[...], preferred_element_type=F32)

        @pl.when(f == 0)
        def _():
            acc[...] = y

        @pl.when(f == MOE_NF - 1)
        def _():
            o_ref[...] = acc[...] + y

    @pl.when(jnp.logical_and(tv_ref[j] == 0, f == MOE_NF - 1))
    def _():
        o_ref[...] = jnp.zeros_like(o_ref)


def _experts(tile_e, tile_src, tile_valid, xs, w13, w2):
    def fidx(j, f, tv):
        return jnp.where(tv[j] == 1, f, MOE_NF - 1)

    return pl.pallas_call(
        _expert_kernel,
        out_shape=jax.ShapeDtypeStruct((MOE_ROWS, D), F32),
        grid_spec=pltpu.PrefetchScalarGridSpec(
            num_scalar_prefetch=3,
            grid=(MOE_TILES, MOE_NF),
            in_specs=[
                pl.BlockSpec((MOE_T, D), lambda j, f, te, ts, tv: (ts[j], 0)),
                pl.BlockSpec((None, D, MOE_FT), lambda j, f, te, ts, tv: (te[j], 0, fidx(j, f, tv))),
                pl.BlockSpec((None, D, MOE_FT),
                             lambda j, f, te, ts, tv: (te[j], 0, MOE_NF + fidx(j, f, tv))),
                pl.BlockSpec((None, MOE_FT, D), lambda j, f, te, ts, tv: (te[j], fidx(j, f, tv), 0)),
            ],
            out_specs=pl.BlockSpec((MOE_T, D), lambda j, f, te, ts, tv: (j, 0)),
            scratch_shapes=[pltpu.VMEM((MOE_T, D), F32)]),
        compiler_params=_cparams(2),
        name="moe_experts",
    )(tile_e, tile_src, tile_valid, xs, w13, w13, w2)


DMA_PT = NP_TOK // DMA_T


def _combine_kernel(pos1_ref, pos2_ref, meta_ref, x_ref, mod_ref, fg_ref, eo_ref,
                    yp_ref, ys_ref, buf, sem):
    i = pl.program_id(0)

    def issue(r, c):
        _row_copy(eo_ref, pos1_ref[r], buf.at[0], r, sem).start()
        _row_copy(eo_ref, pos2_ref[r], buf.at[1], r, sem).start()
        return c

    lax.fori_loop(0, DMA_T, issue, 0)
    for s in range(2):
        pltpu.make_async_copy(eo_ref.at[pl.ds(0, DMA_T)], buf.at[s], sem).wait()

    meta = meta_ref[...]
    g1, g2 = meta[:, 4:5], meta[:, 5:6]
    y = g1 * buf[0] + g2 * buf[1]
    x = x_ref[...] + mod_ref[5:6] * y
    out = x * lax.rsqrt(jnp.mean(x * x, axis=-1, keepdims=True) + EPS) * fg_ref[...]

    @pl.when(i < DMA_PT)
    def _():
        yp_ref[...] = out

    @pl.when(i >= DMA_PT)
    def _():
        ys_ref[...] = out


def _combine(pos1, pos2, meta, x, mods, final_g, eo):
    smem_blk = pl.BlockSpec((DMA_T,), lambda i: (i,), memory_space=pltpu.SMEM)
    return pl.pallas_call(
        _combine_kernel,
        out_shape=(jax.ShapeDtypeStruct((NP_TOK, D), F32),
                   jax.ShapeDtypeStruct((NS_TOK, D), F32)),
        grid=(NTOK // DMA_T,),
        in_specs=[smem_blk, smem_blk,
                  pl.BlockSpec((DMA_T, LANES), lambda i: (i, 0)),
                  pl.BlockSpec((DMA_T, D), lambda i: (i, 0)),
                  _mod_spec(1, DMA_T),
                  _const_spec((1, D)),
                  pl.BlockSpec(memory_space=pl.ANY)],
        out_specs=(pl.BlockSpec((DMA_T, D), lambda i: (jnp.minimum(i, DMA_PT - 1), 0)),
                   pl.BlockSpec((DMA_T, D), lambda i: (jnp.maximum(i - DMA_PT, 0), 0))),
        scratch_shapes=[pltpu.VMEM((2, DMA_T, D), F32), pltpu.SemaphoreType.DMA],
        compiler_params=_cparams(1),
        name="moe_combine",
    )(pos1, pos2, meta, x, mods, final_g, eo)


def _routing_tables(meta, counts):
    cnt = counts[0, :N_EXP].astype(jnp.int32)
    padded = (cnt + MOE_T - 1) // MOE_T * MOE_T
    ends = jnp.cumsum(padded)
    off = ends - padded
    e1 = meta[:, 0].astype(jnp.int32)
    e2 = meta[:, 1].astype(jnp.int32)
    pos1 = off[e1] + meta[:, 2].astype(jnp.int32)
    pos2 = off[e2] + meta[:, 3].astype(jnp.int32)
    n_valid = ends[-1] // MOE_T
    tile = jnp.arange(MOE_TILES, dtype=jnp.int32)
    tile_src = jnp.minimum(tile, n_valid - 1)
    tile_e = jnp.sum((ends[None, :] <= (tile_src * MOE_T)[:, None]).astype(jnp.int32), axis=1)
    tile_e = jnp.minimum(tile_e, N_EXP - 1)
    valid = tile < n_valid
    next_e = jnp.concatenate([tile_e[1:], tile_e[-1:]])
    needs_zero = (tile + 1 >= n_valid) | (next_e != tile_e)
    tile_zero = needs_zero.astype(jnp.int32)
    return pos1, pos2, tile_zero, tile_e, tile_src, valid.astype(jnp.int32)


def kernel(x_prompt, x_sample, cache_k_l1, cache_v_l1, c, c_ctx, ada_w, ada_b, norm_g, conv_w1, conv_b1, conv_dw, conv_dwb, conv_ln_g, conv_ln_b, conv_w2, attn_wqkv, attn_wo, attn_sink, ffn_w13, ffn_w2, moe_router, moe_w13, moe_w2, final_g):
    xp = x_prompt.reshape(NP_TOK, D)
    xs = x_sample.reshape(NS_TOK, D)
    cond = jnp.concatenate([c, c_ctx[None, :], jnp.zeros((N_COND - DEC_BATCH - 1, D), F32)])
    mods = _ada(cond, ada_w, ada_b)

    row = lambda a: a.reshape(1, -1)
    x1 = _conv_layer(xp, xs, mods, row(norm_g[0, 0]), conv_w1.astype(BF16), row(conv_b1),
                     conv_dw, row(conv_dwb), row(conv_ln_g), row(conv_ln_b), conv_w2.astype(BF16))
    x2 = _ffn_layer(x1, mods, row(norm_g[0, 1]), ffn_w13.astype(BF16), ffn_w2.astype(BF16))

    cos, sin = _rope_tables()
    q, k, v, state_k, state_v = _qkv_layer(x2, mods, row(norm_g[1, 0]), attn_wqkv.astype(BF16),
                                           cos, sin)
    o_ctx, o_lat = _attention(q, k, v, cache_k_l1.reshape(DEC_BATCH, PAST, KV_COLS),
                              cache_v_l1.reshape(DEC_BATCH, PAST, KV_COLS), attn_sink)

    router_pad = jnp.pad(moe_router, ((0, 0), (0, LANES - N_EXP)))
    x3, h, meta, counts = _wo_router(o_ctx, o_lat, x2, mods, row(norm_g[1, 1]),
                                     attn_wo.astype(BF16), router_pad)
    pos1, pos2, tile_zero, tile_e, tile_src, tile_valid = _routing_tables(meta, counts)
    xsorted = _dispatch(tile_zero, pos1, pos2, h)
    eo = _experts(tile_e, tile_src, tile_valid, xsorted, moe_w13.astype(BF16), moe_w2.astype(BF16))
    y_p, y_s = _combine(pos1, pos2, meta, x3, mods, row(final_g), eo)

    return (y_p.reshape(BATCH, SEQ, D), y_s.reshape(DEC_BATCH, DEC_SEQ, D),
            state_k.reshape(BATCH, SEQ, N_KV, HD), state_v.reshape(BATCH, SEQ, N_KV, HD))
```

```python
import functools

import jax
import jax.numpy as jnp
from jax import lax
from jax.experimental import pallas as pl
from jax.experimental.pallas import tpu as pltpu

F32 = jnp.float32
BF16 = jnp.bfloat16

D = 1024
BATCH, SEQ = 32, 256
DEC_BATCH, DEC_SEQ = 4, 2048
PAST = 256
NP_TOK = BATCH * SEQ
NS_TOK = DEC_BATCH * DEC_SEQ
NTOK = NP_TOK + NS_TOK
GRID_W = 64
N_HEADS, N_KV, HD = 16, 4, 64
GQA = N_HEADS // N_KV
KV_COLS = N_KV * HD
QKV_COLS = D + 2 * KV_COLS
WINDOW = 128
ROPE_THETA = 10000.0
CONV_W = 31
CONV_PAD = CONV_W // 2
D_FF = 2816
N_EXP = 8
D_FFE = 3584
N_MOD = 6
EPS = 1e-6
NEG_INF = -1e30

N_COND = 8
CTX_ROW = DEC_BATCH

CONV_T = 256
HALO = 16
ROW_T = 512
ATT_T = 256
ATT_WIN = ATT_T + 2 * WINDOW
FF_T = 1408
MOE_T = 512
MOE_FT = 1792
MOE_TILES = 2 * NTOK // MOE_T + N_EXP
MOE_ROWS = MOE_TILES * MOE_T
DMA_T = 1024
LANES = 128

VMEM_LIMIT = 56 * 1024 * 1024


def _cparams(n_axes=1, vmem=VMEM_LIMIT):
    return pltpu.CompilerParams(dimension_semantics=("arbitrary",) * n_axes,
                                vmem_limit_bytes=vmem)


def _silu(x):
    return x * jax.nn.sigmoid(x)


def _norm_mod(x, g, shift, scale):
    y = x * lax.rsqrt(jnp.mean(x * x, axis=-1, keepdims=True) + EPS)
    return (y * g) * (1.0 + scale) + shift


def _cond_row(tile, tile_rows):
    n_prompt_tiles = NP_TOK // tile_rows
    per_seq = DEC_SEQ // tile_rows
    return jnp.where(tile < n_prompt_tiles, CTX_ROW, (tile - n_prompt_tiles) // per_seq)


def _mod_spec(layer, tile_rows):
    return pl.BlockSpec((None, None, N_MOD, D),
                        lambda i, *_: (layer, _cond_row(i, tile_rows), 0, 0))


def _const_spec(shape):
    nd = len(shape)
    return pl.BlockSpec(shape, lambda *_: (0,) * nd)


ADA_TN = 1536


def _ada_kernel(cond_ref, w_ref, b_ref, o_ref):
    s = _silu(cond_ref[...]).astype(BF16)
    o_ref[...] = jnp.dot(s, w_ref[...].astype(BF16), preferred_element_type=F32) + b_ref[...]


def _ada(cond, ada_w, ada_b):
    depth = ada_w.shape[0]
    out = pl.pallas_call(
        _ada_kernel,
        out_shape=jax.ShapeDtypeStruct((depth, N_COND, N_MOD * D), F32),
        grid=(depth, N_MOD * D // ADA_TN),
        in_specs=[pl.BlockSpec((N_COND, D), lambda l, n: (0, 0)),
                  pl.BlockSpec((None, D, ADA_TN), lambda l, n: (l, 0, n)),
                  pl.BlockSpec((None, 1, ADA_TN), lambda l, n: (l, 0, n))],
        out_specs=pl.BlockSpec((None, N_COND, ADA_TN), lambda l, n: (l, 0, n)),
        compiler_params=_cparams(2),
        name="adaln",
    )(cond, ada_w, ada_b.reshape(depth, 1, N_MOD * D))
    return out.reshape(depth, N_COND, N_MOD, D)


CONV_PT = NP_TOK // CONV_T
CONV_TPS = DEC_SEQ // CONV_T
CONV_E = CONV_T + 2 * HALO


def _conv_kernel(xp_ref, xs_ref, xprev_ref, xnext_ref, mod_ref, ng_ref, w1_ref, b1_ref,
                 dw_ref, dwb_ref, lng_ref, lnb_ref, w2_ref, o_ref, g_scr, y_scr):
    i = pl.program_id(0)
    j = i - CONV_PT
    is_prompt = i < CONV_PT
    has_prev = jnp.logical_and(i >= CONV_PT, j % CONV_TPS != 0)
    has_next = jnp.logical_and(i >= CONV_PT, j % CONV_TPS != CONV_TPS - 1)
    x_main = jnp.where(is_prompt, xp_ref[...], xs_ref[...])
    shift, scale, gate = mod_ref[0:1], mod_ref[1:2], mod_ref[2:3]

    xe = jnp.concatenate([xprev_ref[...], x_main, xnext_ref[...]], axis=0)
    h = _norm_mod(xe, ng_ref[...], shift, scale).astype(BF16)
    uv = jnp.dot(h, w1_ref[...], preferred_element_type=F32) + b1_ref[...]
    g = uv[:, :D] * jax.nn.sigmoid(uv[:, D:])
    g_scr[0:HALO] = jnp.where(has_prev, g[0:HALO], 0.0)
    g_scr[HALO:HALO + CONV_T] = g[HALO:HALO + CONV_T]
    g_scr[HALO + CONV_T:] = jnp.where(has_next, g[HALO + CONV_T:], 0.0)

    base = HALO - CONV_PAD
    sub = 8
    for c in range(D // LANES):
        cs = slice(c * LANES, (c + 1) * LANES)
        y = dwb_ref[:, cs]
        for r in range(sub):
            z = None
            for a in range((base + CONV_W - 1) // sub + 1):
                k = sub * a + r - base
                if 0 <= k < CONV_W:
                    term = g_scr[sub * a:sub * a + CONV_T + sub, cs] * dw_ref[k:k + 1, cs]
                    z = term if z is None else z + term
            y = y + z[r:r + CONV_T]
        y_scr[:, cs] = y

    y = y_scr[...]
    mu = jnp.mean(y, axis=-1, keepdims=True)
    d = y - mu
    var = jnp.mean(d * d, axis=-1, keepdims=True)
    yn = d * lax.rsqrt(var + EPS) * lng_ref[...] + lnb_ref[...]
    o = jnp.dot(_silu(yn).astype(BF16), w2_ref[...], preferred_element_type=F32)
    o_ref[...] = x_main + gate * o


def _conv_layer(xp, xs, mods, ng, w1, b1, dw, dwb, lng, lnb, w2):
    n_tiles = NTOK // CONV_T
    hb = CONV_T // HALO
    n_hb = NS_TOK // HALO

    def s_tile(i):
        return jnp.maximum(i - CONV_PT, 0)

    return pl.pallas_call(
        _conv_kernel,
        out_shape=jax.ShapeDtypeStruct((NTOK, D), F32),
        grid=(n_tiles,),
        in_specs=[
            pl.BlockSpec((CONV_T, D), lambda i: (jnp.minimum(i, CONV_PT - 1), 0)),
            pl.BlockSpec((CONV_T, D), lambda i: (s_tile(i), 0)),
            pl.BlockSpec((HALO, D), lambda i: (jnp.maximum(s_tile(i) * hb - 1, 0), 0)),
            pl.BlockSpec((HALO, D), lambda i: (jnp.minimum(s_tile(i) * hb + hb, n_hb - 1), 0)),
            _mod_spec(0, CONV_T),
            _const_spec((1, D)),
            _const_spec((D, 2 * D)),
            _const_spec((1, 2 * D)),
            _const_spec((CONV_W, D)),
            _const_spec((1, D)),
            _const_spec((1, D)),
            _const_spec((1, D)),
            _const_spec((D, D)),
        ],
        out_specs=pl.BlockSpec((CONV_T, D), lambda i: (i, 0)),
        scratch_shapes=[pltpu.VMEM((CONV_E, D), F32), pltpu.VMEM((CONV_T, D), F32)],
        compiler_params=_cparams(1),
        name="conv_mixer",
    )(xp, xs, xs, xs, mods, ng, w1, b1, dw, dwb, lng, lnb, w2)


def _ffn_kernel(x_ref, mod_ref, ng_ref, w13_ref, w2_ref, o_ref):
    x = x_ref[...]
    h = _norm_mod(x, ng_ref[...], mod_ref[3:4], mod_ref[4:5]).astype(BF16)
    acc = jnp.zeros((ROW_T, D), F32)
    for f in range(D_FF // FF_T):
        u = jnp.dot(h, w13_ref[:, f * FF_T:(f + 1) * FF_T], preferred_element_type=F32)
        g = jnp.dot(h, w13_ref[:, D_FF + f * FF_T:D_FF + (f + 1) * FF_T],
                    preferred_element_type=F32)
        a = (_silu(u) * g).astype(BF16)
        acc = acc + jnp.dot(a, w2_ref[f * FF_T:(f + 1) * FF_T, :], preferred_element_type=F32)
    o_ref[...] = x + mod_ref[5:6] * acc


def _ffn_layer(x, mods, ng, w13, w2):
    return pl.pallas_call(
        _ffn_kernel,
        out_shape=jax.ShapeDtypeStruct((NTOK, D), F32),
        grid=(NTOK // ROW_T,),
        in_specs=[pl.BlockSpec((ROW_T, D), lambda i: (i, 0)),
                  _mod_spec(0, ROW_T),
                  _const_spec((1, D)),
                  pl.BlockSpec((D, 2 * D_FF), lambda i: (0, 0), pipeline_mode=pl.Buffered(1)),
                  pl.BlockSpec((D_FF, D), lambda i: (0, 0), pipeline_mode=pl.Buffered(1))],
        out_specs=pl.BlockSpec((ROW_T, D), lambda i: (i, 0)),
        compiler_params=_cparams(1),
        name="dense_ffn",
    )(x, mods, ng, w13, w2)


ROW_PT = NP_TOK // ROW_T
ROW_TPS = DEC_SEQ // ROW_T


def _rope_tables():
    half = HD // 2
    inv = ROPE_THETA ** (-jnp.arange(0, half, 2, dtype=F32) / half)
    t = jnp.arange(DEC_SEQ)
    lane = jnp.arange(LANES)
    jj = lane % HD
    pos = jnp.where(jj[None, :] < half, (t // GRID_W)[:, None], (t % GRID_W)[:, None]).astype(F32)
    ang = pos * inv[jj % (half // 2)][None, :]
    first = (jj % half) < (half // 2)
    cos = jnp.cos(ang)
    sin = jnp.where(first[None, :], -jnp.sin(ang), jnp.sin(ang))
    cos = jnp.concatenate([cos.reshape(ROW_TPS, ROW_T, LANES), jnp.ones((1, ROW_T, LANES), F32)])
    sin = jnp.concatenate([sin.reshape(ROW_TPS, ROW_T, LANES), jnp.zeros((1, ROW_T, LANES), F32)])
    return cos, sin


def _rope(x, cos, sin):
    lane = lax.broadcasted_iota(jnp.int32, x.shape, 1)
    first = (lane % (HD // 2)) < (HD // 4)
    swapped = jnp.where(first, pltpu.roll(x, LANES - HD // 4, axis=1), pltpu.roll(x, HD // 4, axis=1))
    return x * cos + swapped * sin


def _qkv_kernel(x_ref, mod_ref, ng_ref, w_ref, cos_ref, sin_ref,
                q_ref, k_ref, v_ref, sk_ref, sv_ref):
    i = pl.program_id(0)
    h = _norm_mod(x_ref[...], ng_ref[...], mod_ref[0:1], mod_ref[1:2]).astype(BF16)
    y = jnp.dot(h, w_ref[...], preferred_element_type=F32)
    k = y[:, D:D + KV_COLS]
    v = y[:, D + KV_COLS:]

    @pl.when(i < ROW_PT)
    def _():
        sk_ref[...] = k
        sv_ref[...] = v

    cos, sin = cos_ref[...], sin_ref[...]
    scale = HD ** -0.5
    for c in range(D // LANES):
        cs = slice(c * LANES, (c + 1) * LANES)
        q_ref[:, cs] = (_rope(y[:, cs], cos, sin) * scale).astype(BF16)
    for c in range(KV_COLS // LANES):
        cs = slice(c * LANES, (c + 1) * LANES)
        k_ref[:, cs] = _rope(k[:, cs], cos, sin).astype(BF16)
    v_ref[...] = v.astype(BF16)


def _qkv_layer(x, mods, ng, wqkv, cos, sin):
    def tab(i):
        return (jnp.where(i < ROW_PT, ROW_TPS, (i - ROW_PT) % ROW_TPS), 0, 0)

    def state(i):
        return (jnp.minimum(i, ROW_PT - 1), 0)

    return pl.pallas_call(
        _qkv_kernel,
        out_shape=(jax.ShapeDtypeStruct((NTOK, D), BF16),
                   jax.ShapeDtypeStruct((NTOK, KV_COLS), BF16),
                   jax.ShapeDtypeStruct((NTOK, KV_COLS), BF16),
                   jax.ShapeDtypeStruct((NP_TOK, KV_COLS), F32),
                   jax.ShapeDtypeStruct((NP_TOK, KV_COLS), F32)),
        grid=(NTOK // ROW_T,),
        in_specs=[pl.BlockSpec((ROW_T, D), lambda i: (i, 0)),
                  _mod_spec(1, ROW_T),
                  _const_spec((1, D)),
                  _const_spec((D, QKV_COLS)),
                  pl.BlockSpec((None, ROW_T, LANES), tab),
                  pl.BlockSpec((None, ROW_T, LANES), tab)],
        out_specs=(pl.BlockSpec((ROW_T, D), lambda i: (i, 0)),
                   pl.BlockSpec((ROW_T, KV_COLS), lambda i: (i, 0)),
                   pl.BlockSpec((ROW_T, KV_COLS), lambda i: (i, 0)),
                   pl.BlockSpec((ROW_T, KV_COLS), state),
                   pl.BlockSpec((ROW_T, KV_COLS), state)),
        compiler_params=_cparams(1),
        name="qkv_rope",
    )(x, mods, ng, wqkv, cos, sin)


def _attn_heads(q, parts, sink_ref, o_ref):
    nt = (((1,), (1,)), ((), ()))
    for h in range(N_HEADS):
        kv = h // GQA
        hs = slice(kv * HD, (kv + 1) * HD)
        qh = q[:, h * HD:(h + 1) * HD]
        sink = sink_ref[h]
        scores = []
        m = jnp.full((q.shape[0], 1), sink, F32)
        for k, _, mask in parts:
            s = lax.dot_general(qh, k[:, hs], nt, preferred_element_type=F32)
            if mask is not None:
                s = jnp.where(mask, s, NEG_INF)
            scores.append(s)
            m = jnp.maximum(m, jnp.max(s, axis=-1, keepdims=True))
        denom = jnp.exp(sink - m)
        o = jnp.zeros((q.shape[0], HD), F32)
        for s, (_, v, _) in zip(scores, parts):
            p = jnp.exp(s - m)
            denom = denom + jnp.sum(p, axis=-1, keepdims=True)
            o = o + jnp.dot(p.astype(BF16), v[:, hs], preferred_element_type=F32)
        o_ref[:, h * HD:(h + 1) * HD] = (o / denom).astype(o_ref.dtype)


def _ctx_attn_kernel(sink_ref, q_ref, k_ref, v_ref, o_ref):
    _attn_heads(q_ref[...], [(k_ref[...], v_ref[...], None)], sink_ref, o_ref)


def _lat_attn_kernel(sink_ref, q_ref, k_ref, v_ref, ck_ref, cv_ref, o_ref):
    j = pl.program_id(1)
    w0 = pl.multiple_of(jnp.clip(j * ATT_T - WINDOW, 0, DEC_SEQ - ATT_WIN), WINDOW)
    qpos = j * ATT_T + lax.broadcasted_iota(jnp.int32, (ATT_T, ATT_WIN), 0)
    kpos = w0 + lax.broadcasted_iota(jnp.int32, (ATT_T, ATT_WIN), 1)
    mask = jnp.abs(qpos - kpos) <= WINDOW
    parts = [(k_ref[pl.ds(w0, ATT_WIN), :], v_ref[pl.ds(w0, ATT_WIN), :], mask),
             (ck_ref[...].astype(BF16), cv_ref[...].astype(BF16), None)]
    _attn_heads(q_ref[...], parts, sink_ref, o_ref)


def _attention(q, k, v, ck, cv, sink):
    smem = pl.BlockSpec(memory_space=pltpu.SMEM)
    o_ctx = pl.pallas_call(
        _ctx_attn_kernel,
        out_shape=jax.ShapeDtypeStruct((NP_TOK, D), BF16),
        grid=(BATCH,),
        in_specs=[smem,
                  pl.BlockSpec((SEQ, D), lambda b: (b, 0)),
                  pl.BlockSpec((SEQ, KV_COLS), lambda b: (b, 0)),
                  pl.BlockSpec((SEQ, KV_COLS), lambda b: (b, 0))],
        out_specs=pl.BlockSpec((SEQ, D), lambda b: (b, 0)),
        compiler_params=_cparams(1),
        name="ctx_attention",
    )(sink, q, k, v)

    q_off = NP_TOK // ATT_T
    kv_off = NP_TOK // DEC_SEQ
    tps = DEC_SEQ // ATT_T
    o_lat = pl.pallas_call(
        _lat_attn_kernel,
        out_shape=jax.ShapeDtypeStruct((NS_TOK, D), BF16),
        grid=(DEC_BATCH, tps),
        in_specs=[smem,
                  pl.BlockSpec((ATT_T, D), lambda b, j: (q_off + b * tps + j, 0)),
                  pl.BlockSpec((DEC_SEQ, KV_COLS), lambda b, j: (kv_off + b, 0)),
                  pl.BlockSpec((DEC_SEQ, KV_COLS), lambda b, j: (kv_off + b, 0)),
                  pl.BlockSpec((None, PAST, KV_COLS), lambda b, j: (b, 0, 0)),
                  pl.BlockSpec((None, PAST, KV_COLS), lambda b, j: (b, 0, 0))],
        out_specs=pl.BlockSpec((ATT_T, D), lambda b, j: (b * tps + j, 0)),
        compiler_params=_cparams(2),
        name="latent_attention",
    )(sink, q, k, v, ck, cv)
    return o_ctx, o_lat


def _wo_router_kernel(oc_ref, ol_ref, x_ref, mod_ref, ng_ref, wo_ref, r_ref,
                      x3_ref, h_ref, meta_ref, cnt_ref, carry):
    i = pl.program_id(0)

    @pl.when(i == 0)
    def _():
        carry[...] = jnp.zeros_like(carry)

    a = jnp.where(i < ROW_PT, oc_ref[...], ol_ref[...])
    o = jnp.dot(a, wo_ref[...], preferred_element_type=F32)
    x3 = x_ref[...] + mod_ref[2:3] * o
    x3_ref[...] = x3
    h = _norm_mod(x3, ng_ref[...], mod_ref[3:4], mod_ref[4:5])
    h_ref[...] = h

    logit = [jnp.sum(h * r_ref[e:e + 1, :], axis=-1, keepdims=True) for e in range(N_EXP)]

    def top1(cols):
        m = functools.reduce(jnp.maximum, cols)
        idx = jnp.full_like(m, float(N_EXP))
        for e in reversed(range(N_EXP)):
            idx = jnp.where(cols[e] == m, float(e), idx)
        return m, idx

    m1, e1 = top1(logit)
    m2, e2 = top1([jnp.where(e1 == float(e), -jnp.inf, logit[e]) for e in range(N_EXP)])
    lane = lax.broadcasted_iota(jnp.int32, (ROW_T, LANES), 1).astype(F32)
    t = jnp.exp(m2 - m1)
    g1 = 1.0 / (1.0 + t)
    g2 = t / (1.0 + t)

    oh1 = (lane == e1).astype(F32)
    oh2 = (lane == e2).astype(F32)
    oh = oh1 + oh2
    r_i = lax.broadcasted_iota(jnp.int32, (ROW_T, ROW_T), 0)
    c_i = lax.broadcasted_iota(jnp.int32, (ROW_T, ROW_T), 1)
    tri = (c_i < r_i).astype(BF16)
    before = jnp.dot(tri, oh.astype(BF16), preferred_element_type=F32) + carry[0:1, :]
    rank1 = jnp.sum(before * oh1, axis=-1, keepdims=True)
    rank2 = jnp.sum(before * oh2, axis=-1, keepdims=True)
    carry[0:1, :] = carry[0:1, :] + jnp.sum(oh, axis=0, keepdims=True)
    cnt_ref[...] = carry[...]

    meta = jnp.where(lane == 0, e1, 0.0)
    meta = jnp.where(lane == 1, e2, meta)
    meta = jnp.where(lane == 2, rank1, meta)
    meta = jnp.where(lane == 3, rank2, meta)
    meta = jnp.where(lane == 4, g1, meta)
    meta = jnp.where(lane == 5, g2, meta)
    meta_ref[...] = meta


def _wo_router(o_ctx, o_lat, x, mods, ng, wo, router_t):
    return pl.pallas_call(
        _wo_router_kernel,
        out_shape=(jax.ShapeDtypeStruct((NTOK, D), F32),
                   jax.ShapeDtypeStruct((NTOK, D), F32),
                   jax.ShapeDtypeStruct((NTOK, LANES), F32),
                   jax.ShapeDtypeStruct((8, LANES), F32)),
        grid=(NTOK // ROW_T,),
        in_specs=[pl.BlockSpec((ROW_T, D), lambda i: (jnp.minimum(i, ROW_PT - 1), 0)),
                  pl.BlockSpec((ROW_T, D), lambda i: (jnp.maximum(i - ROW_PT, 0), 0)),
                  pl.BlockSpec((ROW_T, D), lambda i: (i, 0)),
                  _mod_spec(1, ROW_T),
                  _const_spec((1, D)),
                  _const_spec((D, D)),
                  _const_spec((N_EXP, D))],
        out_specs=(pl.BlockSpec((ROW_T, D), lambda i: (i, 0)),
                   pl.BlockSpec((ROW_T, D), lambda i: (i, 0)),
                   pl.BlockSpec((ROW_T, LANES), lambda i: (i, 0)),
                   pl.BlockSpec((8, LANES), lambda i: (0, 0))),
        scratch_shapes=[pltpu.VMEM((8, LANES), F32)],
        compiler_params=_cparams(1),
        name="wo_router",
    )(o_ctx, o_lat, x, mods, ng, wo, router_t)


def _row_copy(src, r, dst, p, sem):
    return pltpu.make_async_copy(src.at[pl.ds(r, 1)], dst.at[pl.ds(p, 1)], sem)


def _dispatch_kernel(tz_ref, pos1_ref, pos2_ref, h_ref, xs_ref, zero, sem, zsem):
    @pl.when(pl.program_id(0) == 0)
    def _():
        zero[...] = jnp.zeros_like(zero)

        def tile_copy(t):
            return pltpu.make_async_copy(zero, xs_ref.at[pl.ds(t * MOE_T, MOE_T)], zsem)

        def fill(t, c):
            @pl.when(tz_ref[t] == 1)
            def _():
                tile_copy(t).start()
            return c

        def drain(t, c):
            @pl.when(tz_ref[t] == 1)
            def _():
                tile_copy(t).wait()
            return c

        lax.fori_loop(0, MOE_TILES, fill, 0)
        lax.fori_loop(0, MOE_TILES, drain, 0)

    def issue(r, c):
        _row_copy(h_ref, r, xs_ref, pos1_ref[r], sem).start()
        _row_copy(h_ref, r, xs_ref, pos2_ref[r], sem).start()
        return c

    lax.fori_loop(0, DMA_T, issue, 0)
    for _ in range(2):
        pltpu.make_async_copy(h_ref, xs_ref.at[pl.ds(0, DMA_T)], sem).wait()


def _dispatch(tile_zero, pos1, pos2, h):
    smem_blk = pl.BlockSpec((DMA_T,), lambda i, *_: (i,), memory_space=pltpu.SMEM)
    return pl.pallas_call(
        _dispatch_kernel,
        out_shape=jax.ShapeDtypeStruct((MOE_ROWS, D), F32),
        grid_spec=pltpu.PrefetchScalarGridSpec(
            num_scalar_prefetch=1,
            grid=(NTOK // DMA_T,),
            in_specs=[smem_blk, smem_blk,
                      pl.BlockSpec((DMA_T, D), lambda i, *_: (i, 0))],
            out_specs=pl.BlockSpec(memory_space=pl.ANY),
            scratch_shapes=[pltpu.VMEM((MOE_T, D), F32), pltpu.SemaphoreType.DMA,
                            pltpu.SemaphoreType.DMA]),
        compiler_params=_cparams(1),
        name="moe_dispatch",
    )(tile_zero, pos1, pos2, h)


MOE_NF = D_FFE // MOE_FT


def _expert_kernel(te_ref, ts_ref, tv_ref, x_ref, wu_ref, wg_ref, w2_ref, o_ref, acc):
    j = pl.program_id(0)
    f = pl.program_id(1)

    @pl.when(tv_ref[j] == 1)
    def _():
        x = x_ref[...].astype(BF16)
        u = jnp.dot(x, wu_ref[...], preferred_element_type=F32)
        g = jnp.dot(x, wg_ref[...], preferred_element_type=F32)
        a = (_silu(u) * g).astype(BF16)
        y = jnp.dot(a, w2_ref[...], preferred_element_type=F32)

        @pl.when(f == 0)
        def _():
            acc[...] = y

        @pl.when(f == MOE_NF - 1)
        def _():
            o_ref[...] = acc[...] + y

    @pl.when(jnp.logical_and(tv_ref[j] == 0, f == MOE_NF - 1))
    def _():
        o_ref[...] = jnp.zeros_like(o_ref)


def _experts(tile_e, tile_src, tile_valid, xs, w13, w2):
    def fidx(j, f, tv):
        return jnp.where(tv[j] == 1, f, MOE_NF - 1)

    return pl.pallas_call(
        _expert_kernel,
        out_shape=jax.ShapeDtypeStruct((MOE_ROWS, D), F32),
        grid_spec=pltpu.PrefetchScalarGridSpec(
            num_scalar_prefetch=3,
            grid=(MOE_TILES, MOE_NF),
            in_specs=[
                pl.BlockSpec((MOE_T, D), lambda j, f, te, ts, tv: (ts[j], 0)),
                pl.BlockSpec((None, D, MOE_FT), lambda j, f, te, ts, tv: (te[j], 0, fidx(j, f, tv))),
                pl.BlockSpec((None, D, MOE_FT),
                             lambda j, f, te, ts, tv: (te[j], 0, MOE_NF + fidx(j, f, tv))),
                pl.BlockSpec((None, MOE_FT, D), lambda j, f, te, ts, tv: (te[j], fidx(j, f, tv), 0)),
            ],
            out_specs=pl.BlockSpec((MOE_T, D), lambda j, f, te, ts, tv: (j, 0)),
            scratch_shapes=[pltpu.VMEM((MOE_T, D), F32)]),
        compiler_params=_cparams(2),
        name="moe_experts",
    )(tile_e, tile_src, tile_valid, xs, w13, w13, w2)


DMA_PT = NP_TOK // DMA_T


def _combine_kernel(pos1_ref, pos2_ref, meta_ref, x_ref, mod_ref, fg_ref, eo_ref,
                    yp_ref, ys_ref, buf, sem):
    i = pl.program_id(0)

    def issue(r, c):
        _row_copy(eo_ref, pos1_ref[r], buf.at[0], r, sem).start()
        _row_copy(eo_ref, pos2_ref[r], buf.at[1], r, sem).start()
        return c

    lax.fori_loop(0, DMA_T, issue, 0)
    for s in range(2):
        pltpu.make_async_copy(eo_ref.at[pl.ds(0, DMA_T)], buf.at[s], sem).wait()

    meta = meta_ref[...]
    g1, g2 = meta[:, 4:5], meta[:, 5:6]
    y = g1 * buf[0] + g2 * buf[1]
    x = x_ref[...] + mod_ref[5:6] * y
    out = x * lax.rsqrt(jnp.mean(x * x, axis=-1, keepdims=True) + EPS) * fg_ref[...]

    @pl.when(i < DMA_PT)
    def _():
        yp_ref[...] = out

    @pl.when(i >= DMA_PT)
    def _():
        ys_ref[...] = out


def _combine(pos1, pos2, meta, x, mods, final_g, eo):
    smem_blk = pl.BlockSpec((DMA_T,), lambda i: (i,), memory_space=pltpu.SMEM)
    return pl.pallas_call(
        _combine_kernel,
        out_shape=(jax.ShapeDtypeStruct((NP_TOK, D), F32),
                   jax.ShapeDtypeStruct((NS_TOK, D), F32)),
        grid=(NTOK // DMA_T,),
        in_specs=[smem_blk, smem_blk,
                  pl.BlockSpec((DMA_T, LANES), lambda i: (i, 0)),
                  pl.BlockSpec((DMA_T, D), lambda i: (i, 0)),
                  _mod_spec(1, DMA_T),
                  _const_spec((1, D)),
                  pl.BlockSpec(memory_space=pl.ANY)],
        out_specs=(pl.BlockSpec((DMA_T, D), lambda i: (jnp.minimum(i, DMA_PT - 1), 0)),
                   pl.BlockSpec((DMA_T, D), lambda i: (jnp.maximum(i - DMA_PT, 0), 0))),
        scratch_shapes=[pltpu.VMEM((2, DMA_T, D), F32), pltpu.SemaphoreType.DMA],
        compiler_params=_cparams(1),
        name="moe_combine",
    )(pos1, pos2, meta, x, mods, final_g, eo)


def _routing_tables(meta, counts):
    cnt = counts[0, :N_EXP].astype(jnp.int32)
    padded = (cnt + MOE_T - 1) // MOE_T * MOE_T
    ends = jnp.cumsum(padded)
    off = ends - padded
    e1 = meta[:, 0].astype(jnp.int32)
    e2 = meta[:, 1].astype(jnp.int32)
    pos1 = off[e1] + meta[:, 2].astype(jnp.int32)
    pos2 = off[e2] + meta[:, 3].astype(jnp.int32)
    n_valid = ends[-1] // MOE_T
    tile = jnp.arange(MOE_TILES, dtype=jnp.int32)
    tile_src = jnp.minimum(tile, n_valid - 1)
    tile_e = jnp.sum((ends[None, :] <= (tile_src * MOE_T)[:, None]).astype(jnp.int32), axis=1)
    tile_e = jnp.minimum(tile_e, N_EXP - 1)
    valid = tile < n_valid
    next_e = jnp.concatenate([tile_e[1:], tile_e[-1:]])
    needs_zero = (tile + 1 >= n_valid) | (next_e != tile_e)
    tile_zero = needs_zero.astype(jnp.int32)
    return pos1, pos2, tile_zero, tile_e, tile_src, valid.astype(jnp.int32)


def kernel(x_prompt, x_sample, cache_k_l1, cache_v_l1, c, c_ctx, ada_w, ada_b, norm_g, conv_w1, conv_b1, conv_dw, conv_dwb, conv_ln_g, conv_ln_b, conv_w2, attn_wqkv, attn_wo, attn_sink, ffn_w13, ffn_w2, moe_router, moe_w13, moe_w2, final_g):
    xp = x_prompt.reshape(NP_TOK, D)
    xs = x_sample.reshape(NS_TOK, D)
    cond = jnp.concatenate([c, c_ctx[None, :], jnp.zeros((N_COND - DEC_BATCH - 1, D), F32)])
    mods = _ada(cond, ada_w, ada_b)

    row = lambda a: a.reshape(1, -1)
    x1 = _conv_layer(xp, xs, mods, row(norm_g[0, 0]), conv_w1.astype(BF16), row(conv_b1),
                     conv_dw, row(conv_dwb), row(conv_ln_g), row(conv_ln_b), conv_w2.astype(BF16))
    x2 = _ffn_layer(x1, mods, row(norm_g[0, 1]), ffn_w13.astype(BF16), ffn_w2.astype(BF16))

    cos, sin = _rope_tables()
    q, k, v, state_k, state_v = _qkv_layer(x2, mods, row(norm_g[1, 0]), attn_wqkv.astype(BF16),
                                           cos, sin)
    o_ctx, o_lat = _attention(q, k, v, cache_k_l1.reshape(DEC_BATCH, PAST, KV_COLS),
                              cache_v_l1.reshape(DEC_BATCH, PAST, KV_COLS), attn_sink)

    x3, h, meta, counts = _wo_router(o_ctx, o_lat, x2, mods, row(norm_g[1, 1]),
                                     attn_wo.astype(BF16), moe_router.T)
    pos1, pos2, tile_zero, tile_e, tile_src, tile_valid = _routing_tables(meta, counts)
    xsorted = _dispatch(tile_zero, pos1, pos2, h)
    eo = _experts(tile_e, tile_src, tile_valid, xsorted, moe_w13.astype(BF16), moe_w2.astype(BF16))
    y_p, y_s = _combine(pos1, pos2, meta, x3, mods, row(final_g), eo)

    return (y_p.reshape(BATCH, SEQ, D), y_s.reshape(DEC_BATCH, DEC_SEQ, D),
            state_k.reshape(BATCH, SEQ, N_KV, HD), state_v.reshape(BATCH, SEQ, N_KV, HD))
```

```python
import functools

import jax
import jax.numpy as jnp
from jax import lax
from jax.experimental import pallas as pl
from jax.experimental.pallas import tpu as pltpu

F32 = jnp.float32
BF16 = jnp.bfloat16

D = 1024
BATCH, SEQ = 32, 256
DEC_BATCH, DEC_SEQ = 4, 2048
PAST = 256
NP_TOK = BATCH * SEQ
NS_TOK = DEC_BATCH * DEC_SEQ
NTOK = NP_TOK + NS_TOK
GRID_W = 64
N_HEADS, N_KV, HD = 16, 4, 64
GQA = N_HEADS // N_KV
KV_COLS = N_KV * HD
QKV_COLS = D + 2 * KV_COLS
WINDOW = 128
ROPE_THETA = 10000.0
CONV_W = 31
CONV_PAD = CONV_W // 2
D_FF = 2816
N_EXP = 8
D_FFE = 3584
N_MOD = 6
EPS = 1e-6
NEG_INF = -1e30

N_COND = 8
CTX_ROW = DEC_BATCH

CONV_T = 256
HALO = 16
ROW_T = 512
ATT_T = 256
ATT_WIN = ATT_T + 2 * WINDOW
FF_T = 1408
MOE_T = 512
MOE_FT = 1792
MOE_TILES = 2 * NTOK // MOE_T + N_EXP
MOE_ROWS = MOE_TILES * MOE_T
LANES = 128

VMEM_LIMIT = 56 * 1024 * 1024


def _cparams(n_axes=1, vmem=VMEM_LIMIT):
    return pltpu.CompilerParams(dimension_semantics=("arbitrary",) * n_axes,
                                vmem_limit_bytes=vmem)


def _silu(x):
    return x * jax.nn.sigmoid(x)


def _norm_mod(x, g, shift, scale):
    y = x * lax.rsqrt(jnp.mean(x * x, axis=-1, keepdims=True) + EPS)
    return (y * g) * (1.0 + scale) + shift


def _cond_row(tile, tile_rows):
    n_prompt_tiles = NP_TOK // tile_rows
    per_seq = DEC_SEQ // tile_rows
    return jnp.where(tile < n_prompt_tiles, CTX_ROW, (tile - n_prompt_tiles) // per_seq)


def _mod_spec(layer, tile_rows):
    return pl.BlockSpec((None, None, N_MOD, D),
                        lambda i, *_: (layer, _cond_row(i, tile_rows), 0, 0))


def _const_spec(shape):
    nd = len(shape)
    return pl.BlockSpec(shape, lambda *_: (0,) * nd)


ADA_TN = 1536


def _ada_kernel(cond_ref, w_ref, b_ref, o_ref):
    s = _silu(cond_ref[...]).astype(BF16)
    o_ref[...] = jnp.dot(s, w_ref[...].astype(BF16), preferred_element_type=F32) + b_ref[...]


def _ada(cond, ada_w, ada_b):
    depth = ada_w.shape[0]
    out = pl.pallas_call(
        _ada_kernel,
        out_shape=jax.ShapeDtypeStruct((depth, N_COND, N_MOD * D), F32),
        grid=(depth, N_MOD * D // ADA_TN),
        in_specs=[pl.BlockSpec((N_COND, D), lambda l, n: (0, 0)),
                  pl.BlockSpec((None, D, ADA_TN), lambda l, n: (l, 0, n)),
                  pl.BlockSpec((None, 1, ADA_TN), lambda l, n: (l, 0, n))],
        out_specs=pl.BlockSpec((None, N_COND, ADA_TN), lambda l, n: (l, 0, n)),
        compiler_params=_cparams(2),
        name="adaln",
    )(cond, ada_w, ada_b.reshape(depth, 1, N_MOD * D))
    return out.reshape(depth, N_COND, N_MOD, D)


CONV_PT = NP_TOK // CONV_T
CONV_TPS = DEC_SEQ // CONV_T
CONV_E = CONV_T + 2 * HALO


def _conv_kernel(xp_ref, xs_ref, xprev_ref, xnext_ref, mod_ref, ng_ref, w1_ref, b1_ref,
                 dw_ref, dwb_ref, lng_ref, lnb_ref, w2_ref, o_ref, g_scr, y_scr):
    i = pl.program_id(0)
    j = i - CONV_PT
    is_prompt = i < CONV_PT
    has_prev = jnp.logical_and(i >= CONV_PT, j % CONV_TPS != 0)
    has_next = jnp.logical_and(i >= CONV_PT, j % CONV_TPS != CONV_TPS - 1)
    x_main = jnp.where(is_prompt, xp_ref[...], xs_ref[...])
    shift, scale, gate = mod_ref[0:1], mod_ref[1:2], mod_ref[2:3]

    xe = jnp.concatenate([xprev_ref[...], x_main, xnext_ref[...]], axis=0)
    h = _norm_mod(xe, ng_ref[...], shift, scale).astype(BF16)
    uv = jnp.dot(h, w1_ref[...], preferred_element_type=F32) + b1_ref[...]
    g = uv[:, :D] * jax.nn.sigmoid(uv[:, D:])
    g_scr[0:HALO] = jnp.where(has_prev, g[0:HALO], 0.0)
    g_scr[HALO:HALO + CONV_T] = g[HALO:HALO + CONV_T]
    g_scr[HALO + CONV_T:] = jnp.where(has_next, g[HALO + CONV_T:], 0.0)

    base = HALO - CONV_PAD
    sub = 8
    for c in range(D // LANES):
        cs = slice(c * LANES, (c + 1) * LANES)
        y = dwb_ref[:, cs]
        for r in range(sub):
            z = None
            for a in range((base + CONV_W - 1) // sub + 1):
                k = sub * a + r - base
                if 0 <= k < CONV_W:
                    term = g_scr[sub * a:sub * a + CONV_T + sub, cs] * dw_ref[k:k + 1, cs]
                    z = term if z is None else z + term
            y = y + z[r:r + CONV_T]
        y_scr[:, cs] = y

    y = y_scr[...]
    mu = jnp.mean(y, axis=-1, keepdims=True)
    d = y - mu
    var = jnp.mean(d * d, axis=-1, keepdims=True)
    yn = d * lax.rsqrt(var + EPS) * lng_ref[...] + lnb_ref[...]
    o = jnp.dot(_silu(yn).astype(BF16), w2_ref[...], preferred_element_type=F32)
    o_ref[...] = x_main + gate * o


def _conv_layer(xp, xs, mods, ng, w1, b1, dw, dwb, lng, lnb, w2):
    n_tiles = NTOK // CONV_T
    hb = CONV_T // HALO
    n_hb = NS_TOK // HALO

    def s_tile(i):
        return jnp.maximum(i - CONV_PT, 0)

    return pl.pallas_call(
        _conv_kernel,
        out_shape=jax.ShapeDtypeStruct((NTOK, D), F32),
        grid=(n_tiles,),
        in_specs=[
            pl.BlockSpec((CONV_T, D), lambda i: (jnp.minimum(i, CONV_PT - 1), 0)),
            pl.BlockSpec((CONV_T, D), lambda i: (s_tile(i), 0)),
            pl.BlockSpec((HALO, D), lambda i: (jnp.maximum(s_tile(i) * hb - 1, 0), 0)),
            pl.BlockSpec((HALO, D), lambda i: (jnp.minimum(s_tile(i) * hb + hb, n_hb - 1), 0)),
            _mod_spec(0, CONV_T),
            _const_spec((1, D)),
            _const_spec((D, 2 * D)),
            _const_spec((1, 2 * D)),
            _const_spec((CONV_W, D)),
            _const_spec((1, D)),
            _const_spec((1, D)),
            _const_spec((1, D)),
            _const_spec((D, D)),
        ],
        out_specs=pl.BlockSpec((CONV_T, D), lambda i: (i, 0)),
        scratch_shapes=[pltpu.VMEM((CONV_E, D), F32), pltpu.VMEM((CONV_T, D), F32)],
        compiler_params=_cparams(1),
        name="conv_mixer",
    )(xp, xs, xs, xs, mods, ng, w1, b1, dw, dwb, lng, lnb, w2)


def _ffn_kernel(x_ref, mod_ref, ng_ref, w13_ref, w2_ref, o_ref):
    x = x_ref[...]
    h = _norm_mod(x, ng_ref[...], mod_ref[3:4], mod_ref[4:5]).astype(BF16)
    acc = jnp.zeros((ROW_T, D), F32)
    for f in range(D_FF // FF_T):
        u = jnp.dot(h, w13_ref[:, f * FF_T:(f + 1) * FF_T], preferred_element_type=F32)
        g = jnp.dot(h, w13_ref[:, D_FF + f * FF_T:D_FF + (f + 1) * FF_T],
                    preferred_element_type=F32)
        a = (_silu(u) * g).astype(BF16)
        acc = acc + jnp.dot(a, w2_ref[f * FF_T:(f + 1) * FF_T, :], preferred_element_type=F32)
    o_ref[...] = x + mod_ref[5:6] * acc


def _ffn_layer(x, mods, ng, w13, w2):
    return pl.pallas_call(
        _ffn_kernel,
        out_shape=jax.ShapeDtypeStruct((NTOK, D), F32),
        grid=(NTOK // ROW_T,),
        in_specs=[pl.BlockSpec((ROW_T, D), lambda i: (i, 0)),
                  _mod_spec(0, ROW_T),
                  _const_spec((1, D)),
                  pl.BlockSpec((D, 2 * D_FF), lambda i: (0, 0), pipeline_mode=pl.Buffered(1)),
                  pl.BlockSpec((D_FF, D), lambda i: (0, 0), pipeline_mode=pl.Buffered(1))],
        out_specs=pl.BlockSpec((ROW_T, D), lambda i: (i, 0)),
        compiler_params=_cparams(1),
        name="dense_ffn",
    )(x, mods, ng, w13, w2)


ROW_PT = NP_TOK // ROW_T
ROW_TPS = DEC_SEQ // ROW_T


def _rope_tables():
    half = HD // 2
    inv = ROPE_THETA ** (-jnp.arange(0, half, 2, dtype=F32) / half)
    t = jnp.arange(DEC_SEQ)
    lane = jnp.arange(LANES)
    jj = lane % HD
    pos = jnp.where(jj[None, :] < half, (t // GRID_W)[:, None], (t % GRID_W)[:, None]).astype(F32)
    ang = pos * inv[jj % (half // 2)][None, :]
    first = (jj % half) < (half // 2)
    cos = jnp.cos(ang)
    sin = jnp.where(first[None, :], -jnp.sin(ang), jnp.sin(ang))
    cos = jnp.concatenate([cos.reshape(ROW_TPS, ROW_T, LANES), jnp.ones((1, ROW_T, LANES), F32)])
    sin = jnp.concatenate([sin.reshape(ROW_TPS, ROW_T, LANES), jnp.zeros((1, ROW_T, LANES), F32)])
    return cos, sin


def _rope(x, cos, sin):
    lane = lax.broadcasted_iota(jnp.int32, x.shape, 1)
    first = (lane % (HD // 2)) < (HD // 4)
    swapped = jnp.where(first, pltpu.roll(x, LANES - HD // 4, axis=1), pltpu.roll(x, HD // 4, axis=1))
    return x * cos + swapped * sin


def _qkv_kernel(x_ref, mod_ref, ng_ref, w_ref, cos_ref, sin_ref,
                q_ref, k_ref, v_ref, sk_ref, sv_ref):
    i = pl.program_id(0)
    h = _norm_mod(x_ref[...], ng_ref[...], mod_ref[0:1], mod_ref[1:2]).astype(BF16)
    y = jnp.dot(h, w_ref[...], preferred_element_type=F32)
    k = y[:, D:D + KV_COLS]
    v = y[:, D + KV_COLS:]

    @pl.when(i < ROW_PT)
    def _():
        sk_ref[...] = k
        sv_ref[...] = v

    cos, sin = cos_ref[...], sin_ref[...]
    scale = HD ** -0.5
    for c in range(D // LANES):
        cs = slice(c * LANES, (c + 1) * LANES)
        q_ref[:, cs] = (_rope(y[:, cs], cos, sin) * scale).astype(BF16)
    for c in range(KV_COLS // LANES):
        cs = slice(c * LANES, (c + 1) * LANES)
        k_ref[:, cs] = _rope(k[:, cs], cos, sin).astype(BF16)
    v_ref[...] = v.astype(BF16)


def _qkv_layer(x, mods, ng, wqkv, cos, sin):
    def tab(i):
        return (jnp.where(i < ROW_PT, ROW_TPS, (i - ROW_PT) % ROW_TPS), 0, 0)

    def state(i):
        return (jnp.minimum(i, ROW_PT - 1), 0)

    return pl.pallas_call(
        _qkv_kernel,
        out_shape=(jax.ShapeDtypeStruct((NTOK, D), BF16),
                   jax.ShapeDtypeStruct((NTOK, KV_COLS), BF16),
                   jax.ShapeDtypeStruct((NTOK, KV_COLS), BF16),
                   jax.ShapeDtypeStruct((NP_TOK, KV_COLS), F32),
                   jax.ShapeDtypeStruct((NP_TOK, KV_COLS), F32)),
        grid=(NTOK // ROW_T,),
        in_specs=[pl.BlockSpec((ROW_T, D), lambda i: (i, 0)),
                  _mod_spec(1, ROW_T),
                  _const_spec((1, D)),
                  _const_spec((D, QKV_COLS)),
                  pl.BlockSpec((None, ROW_T, LANES), tab),
                  pl.BlockSpec((None, ROW_T, LANES), tab)],
        out_specs=(pl.BlockSpec((ROW_T, D), lambda i: (i, 0)),
                   pl.BlockSpec((ROW_T, KV_COLS), lambda i: (i, 0)),
                   pl.BlockSpec((ROW_T, KV_COLS), lambda i: (i, 0)),
                   pl.BlockSpec((ROW_T, KV_COLS), state),
                   pl.BlockSpec((ROW_T, KV_COLS), state)),
        compiler_params=_cparams(1),
        name="qkv_rope",
    )(x, mods, ng, wqkv, cos, sin)


def _attn_heads(q, parts, sink_ref, o_ref):
    nt = (((1,), (1,)), ((), ()))
    for h in range(N_HEADS):
        kv = h // GQA
        hs = slice(kv * HD, (kv + 1) * HD)
        qh = q[:, h * HD:(h + 1) * HD]
        sink = sink_ref[h]
        scores = []
        m = jnp.full((q.shape[0], 1), sink, F32)
        for k, _, mask in parts:
            s = lax.dot_general(qh, k[:, hs], nt, preferred_element_type=F32)
            if mask is not None:
                s = jnp.where(mask, s, NEG_INF)
            scores.append(s)
            m = jnp.maximum(m, jnp.max(s, axis=-1, keepdims=True))
        denom = jnp.exp(sink - m)
        o = jnp.zeros((q.shape[0], HD), F32)
        for s, (_, v, _) in zip(scores, parts):
            p = jnp.exp(s - m)
            denom = denom + jnp.sum(p, axis=-1, keepdims=True)
            o = o + jnp.dot(p.astype(BF16), v[:, hs], preferred_element_type=F32)
        o_ref[:, h * HD:(h + 1) * HD] = (o / denom).astype(o_ref.dtype)


def _ctx_attn_kernel(sink_ref, q_ref, k_ref, v_ref, o_ref):
    _attn_heads(q_ref[...], [(k_ref[...], v_ref[...], None)], sink_ref, o_ref)


def _lat_attn_kernel(sink_ref, q_ref, k_ref, v_ref, ck_ref, cv_ref, o_ref):
    j = pl.program_id(1)
    w0 = pl.multiple_of(jnp.clip(j * ATT_T - WINDOW, 0, DEC_SEQ - ATT_WIN), WINDOW)
    qpos = j * ATT_T + lax.broadcasted_iota(jnp.int32, (ATT_T, ATT_WIN), 0)
    kpos = w0 + lax.broadcasted_iota(jnp.int32, (ATT_T, ATT_WIN), 1)
    mask = jnp.abs(qpos - kpos) <= WINDOW
    parts = [(k_ref[pl.ds(w0, ATT_WIN), :], v_ref[pl.ds(w0, ATT_WIN), :], mask),
             (ck_ref[...].astype(BF16), cv_ref[...].astype(BF16), None)]
    _attn_heads(q_ref[...], parts, sink_ref, o_ref)


def _attention(q, k, v, ck, cv, sink):
    smem = pl.BlockSpec(memory_space=pltpu.SMEM)
    o_ctx = pl.pallas_call(
        _ctx_attn_kernel,
        out_shape=jax.ShapeDtypeStruct((NP_TOK, D), BF16),
        grid=(BATCH,),
        in_specs=[smem,
                  pl.BlockSpec((SEQ, D), lambda b: (b, 0)),
                  pl.BlockSpec((SEQ, KV_COLS), lambda b: (b, 0)),
                  pl.BlockSpec((SEQ, KV_COLS), lambda b: (b, 0))],
        out_specs=pl.BlockSpec((SEQ, D), lambda b: (b, 0)),
        compiler_params=_cparams(1),
        name="ctx_attention",
    )(sink, q, k, v)

    q_off = NP_TOK // ATT_T
    kv_off = NP_TOK // DEC_SEQ
    tps = DEC_SEQ // ATT_T
    o_lat = pl.pallas_call(
        _lat_attn_kernel,
        out_shape=jax.ShapeDtypeStruct((NS_TOK, D), BF16),
        grid=(DEC_BATCH, tps),
        in_specs=[smem,
                  pl.BlockSpec((ATT_T, D), lambda b, j: (q_off + b * tps + j, 0)),
                  pl.BlockSpec((DEC_SEQ, KV_COLS), lambda b, j: (kv_off + b, 0)),
                  pl.BlockSpec((DEC_SEQ, KV_COLS), lambda b, j: (kv_off + b, 0)),
                  pl.BlockSpec((None, PAST, KV_COLS), lambda b, j: (b, 0, 0)),
                  pl.BlockSpec((None, PAST, KV_COLS), lambda b, j: (b, 0, 0))],
        out_specs=pl.BlockSpec((ATT_T, D), lambda b, j: (b * tps + j, 0)),
        compiler_params=_cparams(2),
        name="latent_attention",
    )(sink, q, k, v, ck, cv)
    return o_ctx, o_lat


def _wo_router_kernel(oc_ref, ol_ref, x_ref, mod_ref, ng_ref, wo_ref, r_ref,
                      x3_ref, h_ref, meta_ref, cnt_ref, carry):
    i = pl.program_id(0)

    @pl.when(i == 0)
    def _():
        carry[...] = jnp.zeros_like(carry)

    a = jnp.where(i < ROW_PT, oc_ref[...], ol_ref[...])
    o = jnp.dot(a, wo_ref[...], preferred_element_type=F32)
    x3 = x_ref[...] + mod_ref[2:3] * o
    x3_ref[...] = x3
    h = _norm_mod(x3, ng_ref[...], mod_ref[3:4], mod_ref[4:5])
    h_ref[...] = h

    logit = [jnp.sum(h * r_ref[e:e + 1, :], axis=-1, keepdims=True) for e in range(N_EXP)]

    def top1(cols):
        m = functools.reduce(jnp.maximum, cols)
        idx = jnp.full_like(m, float(N_EXP))
        for e in reversed(range(N_EXP)):
            idx = jnp.where(cols[e] == m, float(e), idx)
        return m, idx

    m1, e1 = top1(logit)
    m2, e2 = top1([jnp.where(e1 == float(e), -jnp.inf, logit[e]) for e in range(N_EXP)])
    lane = lax.broadcasted_iota(jnp.int32, (ROW_T, LANES), 1).astype(F32)
    t = jnp.exp(m2 - m1)
    g1 = 1.0 / (1.0 + t)
    g2 = t / (1.0 + t)

    oh1 = (lane == e1).astype(F32)
    oh2 = (lane == e2).astype(F32)
    oh = oh1 + oh2
    r_i = lax.broadcasted_iota(jnp.int32, (ROW_T, ROW_T), 0)
    c_i = lax.broadcasted_iota(jnp.int32, (ROW_T, ROW_T), 1)
    tri = (c_i < r_i).astype(BF16)
    before = jnp.dot(tri, oh.astype(BF16), preferred_element_type=F32) + carry[0:1, :]
    rank1 = jnp.sum(before * oh1, axis=-1, keepdims=True)
    rank2 = jnp.sum(before * oh2, axis=-1, keepdims=True)
    carry[0:1, :] = carry[0:1, :] + jnp.sum(oh, axis=0, keepdims=True)
    cnt_ref[...] = carry[...]

    meta = jnp.where(lane == 0, e1, 0.0)
    meta = jnp.where(lane == 1, e2, meta)
    meta = jnp.where(lane == 2, rank1, meta)
    meta = jnp.where(lane == 3, rank2, meta)
    meta = jnp.where(lane == 4, g1, meta)
    meta = jnp.where(lane == 5, g2, meta)
    meta_ref[...] = meta


def _wo_router(o_ctx, o_lat, x, mods, ng, wo, router_t):
    return pl.pallas_call(
        _wo_router_kernel,
        out_shape=(jax.ShapeDtypeStruct((NTOK, D), F32),
                   jax.ShapeDtypeStruct((NTOK, D), F32),
                   jax.ShapeDtypeStruct((NTOK, LANES), F32),
                   jax.ShapeDtypeStruct((8, LANES), F32)),
        grid=(NTOK // ROW_T,),
        in_specs=[pl.BlockSpec((ROW_T, D), lambda i: (jnp.minimum(i, ROW_PT - 1), 0)),
                  pl.BlockSpec((ROW_T, D), lambda i: (jnp.maximum(i - ROW_PT, 0), 0)),
                  pl.BlockSpec((ROW_T, D), lambda i: (i, 0)),
                  _mod_spec(1, ROW_T),
                  _const_spec((1, D)),
                  _const_spec((D, D)),
                  _const_spec((N_EXP, D))],
        out_specs=(pl.BlockSpec((ROW_T, D), lambda i: (i, 0)),
                   pl.BlockSpec((ROW_T, D), lambda i: (i, 0)),
                   pl.BlockSpec((ROW_T, LANES), lambda i: (i, 0)),
                   pl.BlockSpec((8, LANES), lambda i: (0, 0))),
        scratch_shapes=[pltpu.VMEM((8, LANES), F32)],
        compiler_params=_cparams(1),
        name="wo_router",
    )(o_ctx, o_lat, x, mods, ng, wo, router_t)


MOE_NF = D_FFE // MOE_FT
MOE_PART = MOE_T // MOE_NF
MOE_STEPS = MOE_TILES + 1
Y_SPARE = N_EXP * MOE_T
Y_ROWS = 2 * NTOK + Y_SPARE


def _expert_kernel(te_ref, tv_ref, src_ref, dst_ref, h_ref, wu_ref, wg_ref, w2_ref, y_ref,
                   xbuf, obuf, gsem, ssem):
    j = pl.program_id(0)
    f = pl.program_id(1)
    cur = j % 2
    nxt = 1 - cur
    computes = tv_ref[j + 1] == 1
    drains = tv_ref[j] == 1

    def gather(tile, slot, r):
        return pltpu.make_async_copy(h_ref.at[pl.ds(src_ref[tile * MOE_T + r], 1)],
                                     xbuf.at[slot, pl.ds(r, 1)], gsem.at[slot])

    def gather_all(slot):
        return pltpu.make_async_copy(h_ref.at[pl.ds(0, MOE_T)], xbuf.at[slot], gsem.at[slot])

    def scatter(tile, slot, r):
        return pltpu.make_async_copy(obuf.at[slot, pl.ds(r, 1)],
                                     y_ref.at[pl.ds(dst_ref[tile * MOE_T + r], 1)], ssem.at[slot])

    def scatter_all(slot):
        return pltpu.make_async_copy(obuf.at[slot], y_ref.at[pl.ds(0, MOE_T)], ssem.at[slot])

    @pl.when(jnp.logical_and(j == 0, f == 0))
    def _():
        obuf[...] = jnp.zeros_like(obuf)
        for s in range(N_EXP):
            spare = pltpu.make_async_copy(
                obuf.at[0], y_ref.at[pl.ds(2 * NTOK + s * MOE_T, MOE_T)], ssem.at[0])
            spare.start()
            spare.wait()

        def first(r, c):
            gather(0, 0, r).start()
            return c

        lax.fori_loop(0, MOE_T, first, 0)
        gather_all(0).wait()

    def step(compute, drain):
        base = f * MOE_PART
        if compute:
            for i in range(MOE_PART):
                gather(j + 1, nxt, base + i).start()
        if drain:
            for i in range(MOE_PART):
                scatter(j - 1, nxt, base + i).start()
        if compute:
            x = xbuf[cur].astype(BF16)
            u = jnp.dot(x, wu_ref[...], preferred_element_type=F32)
            g = jnp.dot(x, wg_ref[...], preferred_element_type=F32)
            a = (_silu(u) * g).astype(BF16)
            y = jnp.dot(a, w2_ref[...], preferred_element_type=F32)
            obuf[cur] = jnp.where(f == 0, 0.0, obuf[cur]) + y

        @pl.when(f == MOE_NF - 1)
        def _():
            if compute:
                gather_all(nxt).wait()
            if drain:
                scatter_all(nxt).wait()

    @pl.when(jnp.logical_and(computes, drains))
    def _():
        step(True, True)

    @pl.when(jnp.logical_and(computes, jnp.logical_not(drains)))
    def _():
        step(True, False)

    @pl.when(jnp.logical_and(jnp.logical_not(computes), drains))
    def _():
        step(False, True)


def _experts(tile_e, tile_v, src, dst, h, w13, w2):
    def fidx(j, f, tv):
        return jnp.where(tv[j + 1] == 1, f, MOE_NF - 1)

    return pl.pallas_call(
        _expert_kernel,
        out_shape=jax.ShapeDtypeStruct((Y_ROWS, D), F32),
        grid_spec=pltpu.PrefetchScalarGridSpec(
            num_scalar_prefetch=4,
            grid=(MOE_STEPS, MOE_NF),
            in_specs=[
                pl.BlockSpec(memory_space=pl.ANY),
                pl.BlockSpec((None, D, MOE_FT), lambda j, f, te, tv, *_: (te[j], 0, fidx(j, f, tv))),
                pl.BlockSpec((None, D, MOE_FT),
                             lambda j, f, te, tv, *_: (te[j], 0, MOE_NF + fidx(j, f, tv))),
                pl.BlockSpec((None, MOE_FT, D), lambda j, f, te, tv, *_: (te[j], fidx(j, f, tv), 0)),
            ],
            out_specs=pl.BlockSpec(memory_space=pl.ANY),
            scratch_shapes=[pltpu.VMEM((2, MOE_T, D), F32), pltpu.VMEM((2, MOE_T, D), F32),
                            pltpu.SemaphoreType.DMA((2,)), pltpu.SemaphoreType.DMA((2,))]),
        compiler_params=_cparams(2),
        name="moe_experts",
    )(tile_e, tile_v, src, dst, h, w13, w13, w2)


def _combine_kernel(y1_ref, y2_ref, meta_ref, x_ref, mod_ref, fg_ref, yp_ref, ys_ref):
    i = pl.program_id(0)
    meta = meta_ref[...]
    y = meta[:, 4:5] * y1_ref[...] + meta[:, 5:6] * y2_ref[...]
    x = x_ref[...] + mod_ref[5:6] * y
    out = x * lax.rsqrt(jnp.mean(x * x, axis=-1, keepdims=True) + EPS) * fg_ref[...]

    @pl.when(i < ROW_PT)
    def _():
        yp_ref[...] = out

    @pl.when(i >= ROW_PT)
    def _():
        ys_ref[...] = out


def _combine(y, meta, x, mods, final_g):
    slot2 = NTOK // ROW_T
    return pl.pallas_call(
        _combine_kernel,
        out_shape=(jax.ShapeDtypeStruct((NP_TOK, D), F32),
                   jax.ShapeDtypeStruct((NS_TOK, D), F32)),
        grid=(NTOK // ROW_T,),
        in_specs=[pl.BlockSpec((ROW_T, D), lambda i: (i, 0)),
                  pl.BlockSpec((ROW_T, D), lambda i: (slot2 + i, 0)),
                  pl.BlockSpec((ROW_T, LANES), lambda i: (i, 0)),
                  pl.BlockSpec((ROW_T, D), lambda i: (i, 0)),
                  _mod_spec(1, ROW_T),
                  _const_spec((1, D))],
        out_specs=(pl.BlockSpec((ROW_T, D), lambda i: (jnp.minimum(i, ROW_PT - 1), 0)),
                   pl.BlockSpec((ROW_T, D), lambda i: (jnp.maximum(i - ROW_PT, 0), 0))),
        compiler_params=_cparams(1),
        name="moe_combine",
    )(y, y, meta, x, mods, final_g)


def _routing_tables(meta, counts):
    cnt = counts[0, :N_EXP].astype(jnp.int32)
    padded = (cnt + MOE_T - 1) // MOE_T * MOE_T
    ends = jnp.cumsum(padded)
    off = ends - padded
    e1 = meta[:, 0].astype(jnp.int32)
    e2 = meta[:, 1].astype(jnp.int32)
    pos = jnp.concatenate([off[e1] + meta[:, 2].astype(jnp.int32),
                           off[e2] + meta[:, 3].astype(jnp.int32)])
    n_valid = ends[-1] // MOE_T
    tile = jnp.arange(MOE_STEPS, dtype=jnp.int32)
    first_row = jnp.minimum(tile, n_valid - 1) * MOE_T
    tile_e = jnp.sum((ends[None, :] <= first_row[:, None]).astype(jnp.int32), axis=1)
    tile_e = jnp.minimum(tile_e, N_EXP - 1)
    tile_v = jnp.concatenate([jnp.zeros((1,), jnp.int32), (tile < n_valid).astype(jnp.int32)])

    tok = jnp.arange(NTOK, dtype=jnp.int32)
    src = jnp.zeros((MOE_ROWS + MOE_T,), jnp.int32).at[pos].set(jnp.concatenate([tok, tok]))
    row = jnp.arange(MOE_ROWS, dtype=jnp.int32)
    row_e = tile_e[row // MOE_T]
    spare = 2 * NTOK + row_e * MOE_T + jnp.clip(row - (off + cnt)[row_e], 0, MOE_T - 1)
    dst = spare.at[pos].set(jnp.concatenate([tok, NTOK + tok]))
    return tile_e, tile_v, src, dst


def kernel(x_prompt, x_sample, cache_k_l1, cache_v_l1, c, c_ctx, ada_w, ada_b, norm_g, conv_w1, conv_b1, conv_dw, conv_dwb, conv_ln_g, conv_ln_b, conv_w2, attn_wqkv, attn_wo, attn_sink, ffn_w13, ffn_w2, moe_router, moe_w13, moe_w2, final_g):
    xp = x_prompt.reshape(NP_TOK, D)
    xs = x_sample.reshape(NS_TOK, D)
    cond = jnp.concatenate([c, c_ctx[None, :], jnp.zeros((N_COND - DEC_BATCH - 1, D), F32)])
    mods = _ada(cond, ada_w, ada_b)

    row = lambda a: a.reshape(1, -1)
    x1 = _conv_layer(xp, xs, mods, row(norm_g[0, 0]), conv_w1.astype(BF16), row(conv_b1),
                     conv_dw, row(conv_dwb), row(conv_ln_g), row(conv_ln_b), conv_w2.astype(BF16))
    x2 = _ffn_layer(x1, mods, row(norm_g[0, 1]), ffn_w13.astype(BF16), ffn_w2.astype(BF16))

    cos, sin = _rope_tables()
    q, k, v, state_k, state_v = _qkv_layer(x2, mods, row(norm_g[1, 0]), attn_wqkv.astype(BF16),
                                           cos, sin)
    o_ctx, o_lat = _attention(q, k, v, cache_k_l1.reshape(DEC_BATCH, PAST, KV_COLS),
                              cache_v_l1.reshape(DEC_BATCH, PAST, KV_COLS), attn_sink)

    x3, h, meta, counts = _wo_router(o_ctx, o_lat, x2, mods, row(norm_g[1, 1]),
                                     attn_wo.astype(BF16), moe_router.T)
    tile_e, tile_v, src, dst = _routing_tables(meta, counts)
    y = _experts(tile_e, tile_v, src, dst, h, moe_w13.astype(BF16), moe_w2.astype(BF16))
    y_p, y_s = _combine(y, meta, x3, mods, row(final_g))

    return (y_p.reshape(BATCH, SEQ, D), y_s.reshape(DEC_BATCH, DEC_SEQ, D),
            state_k.reshape(BATCH, SEQ, N_KV, HD), state_v.reshape(BATCH, SEQ, N_KV, HD))
```

```python
import functools

import jax
import jax.numpy as jnp
from jax import lax
from jax.experimental import pallas as pl
from jax.experimental.pallas import tpu as pltpu

F32 = jnp.float32
BF16 = jnp.bfloat16

D = 1024
BATCH, SEQ = 32, 256
DEC_BATCH, DEC_SEQ = 4, 2048
PAST = 256
NP_TOK = BATCH * SEQ
NS_TOK = DEC_BATCH * DEC_SEQ
NTOK = NP_TOK + NS_TOK
GRID_W = 64
N_HEADS, N_KV, HD = 16, 4, 64
GQA = N_HEADS // N_KV
KV_COLS = N_KV * HD
QKV_COLS = D + 2 * KV_COLS
WINDOW = 128
ROPE_THETA = 10000.0
CONV_W = 31
CONV_PAD = CONV_W // 2
D_FF = 2816
N_EXP = 8
D_FFE = 3584
N_MOD = 6
EPS = 1e-6
NEG_INF = -1e30

N_COND = 8
CTX_ROW = DEC_BATCH

CONV_T = 256
HALO = 16
ROW_T = 512
ATT_T = 256
ATT_WIN = ATT_T + 2 * WINDOW
FF_T = 1408
MOE_T = 512
MOE_FT = 1792
MOE_TILES = 2 * NTOK // MOE_T + N_EXP
MOE_ROWS = MOE_TILES * MOE_T
LANES = 128

VMEM_LIMIT = 56 * 1024 * 1024


def _cparams(n_axes=1, vmem=VMEM_LIMIT):
    return pltpu.CompilerParams(dimension_semantics=("arbitrary",) * n_axes,
                                vmem_limit_bytes=vmem)


def _silu(x):
    return x * jax.nn.sigmoid(x)


def _norm_mod(x, g, shift, scale):
    y = x * lax.rsqrt(jnp.mean(x * x, axis=-1, keepdims=True) + EPS)
    return (y * g) * (1.0 + scale) + shift


def _cond_row(tile, tile_rows):
    n_prompt_tiles = NP_TOK // tile_rows
    per_seq = DEC_SEQ // tile_rows
    return jnp.where(tile < n_prompt_tiles, CTX_ROW, (tile - n_prompt_tiles) // per_seq)


def _mod_spec(layer, tile_rows):
    return pl.BlockSpec((None, None, N_MOD, D),
                        lambda i, *_: (layer, _cond_row(i, tile_rows), 0, 0))


def _const_spec(shape):
    nd = len(shape)
    return pl.BlockSpec(shape, lambda *_: (0,) * nd)


ADA_TN = 1536


def _ada_kernel(cond_ref, w_ref, b_ref, o_ref):
    s = _silu(cond_ref[...]).astype(BF16)
    o_ref[...] = jnp.dot(s, w_ref[...].astype(BF16), preferred_element_type=F32) + b_ref[...]


def _ada(cond, ada_w, ada_b):
    depth = ada_w.shape[0]
    out = pl.pallas_call(
        _ada_kernel,
        out_shape=jax.ShapeDtypeStruct((depth, N_COND, N_MOD * D), F32),
        grid=(depth, N_MOD * D // ADA_TN),
        in_specs=[pl.BlockSpec((N_COND, D), lambda l, n: (0, 0)),
                  pl.BlockSpec((None, D, ADA_TN), lambda l, n: (l, 0, n)),
                  pl.BlockSpec((None, 1, ADA_TN), lambda l, n: (l, 0, n))],
        out_specs=pl.BlockSpec((None, N_COND, ADA_TN), lambda l, n: (l, 0, n)),
        compiler_params=_cparams(2),
        name="adaln",
    )(cond, ada_w, ada_b.reshape(depth, 1, N_MOD * D))
    return out.reshape(depth, N_COND, N_MOD, D)


CONV_PT = NP_TOK // CONV_T
CONV_TPS = DEC_SEQ // CONV_T
CONV_E = CONV_T + 2 * HALO


def _conv_kernel(xp_ref, xs_ref, xprev_ref, xnext_ref, mod_ref, ng_ref, w1_ref, b1_ref,
                 dw_ref, dwb_ref, lng_ref, lnb_ref, w2_ref, o_ref, g_scr, y_scr):
    i = pl.program_id(0)
    j = i - CONV_PT
    is_prompt = i < CONV_PT
    has_prev = jnp.logical_and(i >= CONV_PT, j % CONV_TPS != 0)
    has_next = jnp.logical_and(i >= CONV_PT, j % CONV_TPS != CONV_TPS - 1)
    x_main = jnp.where(is_prompt, xp_ref[...], xs_ref[...])
    shift, scale, gate = mod_ref[0:1], mod_ref[1:2], mod_ref[2:3]

    xe = jnp.concatenate([xprev_ref[...], x_main, xnext_ref[...]], axis=0)
    h = _norm_mod(xe, ng_ref[...], shift, scale).astype(BF16)
    uv = jnp.dot(h, w1_ref[...], preferred_element_type=F32) + b1_ref[...]
    g = uv[:, :D] * jax.nn.sigmoid(uv[:, D:])
    g_scr[0:HALO] = jnp.where(has_prev, g[0:HALO], 0.0)
    g_scr[HALO:HALO + CONV_T] = g[HALO:HALO + CONV_T]
    g_scr[HALO + CONV_T:] = jnp.where(has_next, g[HALO + CONV_T:], 0.0)

    base = HALO - CONV_PAD
    sub = 8
    for c in range(D // LANES):
        cs = slice(c * LANES, (c + 1) * LANES)
        y = dwb_ref[:, cs]
        for r in range(sub):
            z = None
            for a in range((base + CONV_W - 1) // sub + 1):
                k = sub * a + r - base
                if 0 <= k < CONV_W:
                    term = g_scr[sub * a:sub * a + CONV_T + sub, cs] * dw_ref[k:k + 1, cs]
                    z = term if z is None else z + term
            y = y + z[r:r + CONV_T]
        y_scr[:, cs] = y

    y = y_scr[...]
    mu = jnp.mean(y, axis=-1, keepdims=True)
    d = y - mu
    var = jnp.mean(d * d, axis=-1, keepdims=True)
    yn = d * lax.rsqrt(var + EPS) * lng_ref[...] + lnb_ref[...]
    o = jnp.dot(_silu(yn).astype(BF16), w2_ref[...], preferred_element_type=F32)
    o_ref[...] = x_main + gate * o


def _conv_layer(xp, xs, mods, ng, w1, b1, dw, dwb, lng, lnb, w2):
    n_tiles = NTOK // CONV_T
    hb = CONV_T // HALO
    n_hb = NS_TOK // HALO

    def s_tile(i):
        return jnp.maximum(i - CONV_PT, 0)

    return pl.pallas_call(
        _conv_kernel,
        out_shape=jax.ShapeDtypeStruct((NTOK, D), F32),
        grid=(n_tiles,),
        in_specs=[
            pl.BlockSpec((CONV_T, D), lambda i: (jnp.minimum(i, CONV_PT - 1), 0)),
            pl.BlockSpec((CONV_T, D), lambda i: (s_tile(i), 0)),
            pl.BlockSpec((HALO, D), lambda i: (jnp.maximum(s_tile(i) * hb - 1, 0), 0)),
            pl.BlockSpec((HALO, D), lambda i: (jnp.minimum(s_tile(i) * hb + hb, n_hb - 1), 0)),
            _mod_spec(0, CONV_T),
            _const_spec((1, D)),
            _const_spec((D, 2 * D)),
            _const_spec((1, 2 * D)),
            _const_spec((CONV_W, D)),
            _const_spec((1, D)),
            _const_spec((1, D)),
            _const_spec((1, D)),
            _const_spec((D, D)),
        ],
        out_specs=pl.BlockSpec((CONV_T, D), lambda i: (i, 0)),
        scratch_shapes=[pltpu.VMEM((CONV_E, D), F32), pltpu.VMEM((CONV_T, D), F32)],
        compiler_params=_cparams(1),
        name="conv_mixer",
    )(xp, xs, xs, xs, mods, ng, w1, b1, dw, dwb, lng, lnb, w2)


def _ffn_kernel(x_ref, mod_ref, ng_ref, w13_ref, w2_ref, o_ref):
    x = x_ref[...]
    h = _norm_mod(x, ng_ref[...], mod_ref[3:4], mod_ref[4:5]).astype(BF16)
    acc = jnp.zeros((ROW_T, D), F32)
    for f in range(D_FF // FF_T):
        u = jnp.dot(h, w13_ref[:, f * FF_T:(f + 1) * FF_T], preferred_element_type=F32)
        g = jnp.dot(h, w13_ref[:, D_FF + f * FF_T:D_FF + (f + 1) * FF_T],
                    preferred_element_type=F32)
        a = (_silu(u) * g).astype(BF16)
        acc = acc + jnp.dot(a, w2_ref[f * FF_T:(f + 1) * FF_T, :], preferred_element_type=F32)
    o_ref[...] = x + mod_ref[5:6] * acc


def _ffn_layer(x, mods, ng, w13, w2):
    return pl.pallas_call(
        _ffn_kernel,
        out_shape=jax.ShapeDtypeStruct((NTOK, D), F32),
        grid=(NTOK // ROW_T,),
        in_specs=[pl.BlockSpec((ROW_T, D), lambda i: (i, 0)),
                  _mod_spec(0, ROW_T),
                  _const_spec((1, D)),
                  pl.BlockSpec((D, 2 * D_FF), lambda i: (0, 0), pipeline_mode=pl.Buffered(1)),
                  pl.BlockSpec((D_FF, D), lambda i: (0, 0), pipeline_mode=pl.Buffered(1))],
        out_specs=pl.BlockSpec((ROW_T, D), lambda i: (i, 0)),
        compiler_params=_cparams(1),
        name="dense_ffn",
    )(x, mods, ng, w13, w2)


ROW_PT = NP_TOK // ROW_T
ROW_TPS = DEC_SEQ // ROW_T


def _rope_tables():
    half = HD // 2
    inv = ROPE_THETA ** (-jnp.arange(0, half, 2, dtype=F32) / half)
    t = jnp.arange(DEC_SEQ)
    lane = jnp.arange(LANES)
    jj = lane % HD
    pos = jnp.where(jj[None, :] < half, (t // GRID_W)[:, None], (t % GRID_W)[:, None]).astype(F32)
    ang = pos * inv[jj % (half // 2)][None, :]
    first = (jj % half) < (half // 2)
    cos = jnp.cos(ang)
    sin = jnp.where(first[None, :], -jnp.sin(ang), jnp.sin(ang))
    cos = jnp.concatenate([cos.reshape(ROW_TPS, ROW_T, LANES), jnp.ones((1, ROW_T, LANES), F32)])
    sin = jnp.concatenate([sin.reshape(ROW_TPS, ROW_T, LANES), jnp.zeros((1, ROW_T, LANES), F32)])
    return cos, sin


def _rope(x, cos, sin):
    lane = lax.broadcasted_iota(jnp.int32, x.shape, 1)
    first = (lane % (HD // 2)) < (HD // 4)
    swapped = jnp.where(first, pltpu.roll(x, LANES - HD // 4, axis=1), pltpu.roll(x, HD // 4, axis=1))
    return x * cos + swapped * sin


def _qkv_kernel(x_ref, mod_ref, ng_ref, w_ref, cos_ref, sin_ref,
                q_ref, k_ref, v_ref, sk_ref, sv_ref):
    i = pl.program_id(0)
    h = _norm_mod(x_ref[...], ng_ref[...], mod_ref[0:1], mod_ref[1:2]).astype(BF16)
    y = jnp.dot(h, w_ref[...], preferred_element_type=F32)
    k = y[:, D:D + KV_COLS]
    v = y[:, D + KV_COLS:]

    @pl.when(i < ROW_PT)
    def _():
        sk_ref[...] = k
        sv_ref[...] = v

    cos, sin = cos_ref[...], sin_ref[...]
    scale = HD ** -0.5
    for c in range(D // LANES):
        cs = slice(c * LANES, (c + 1) * LANES)
        q_ref[:, cs] = (_rope(y[:, cs], cos, sin) * scale).astype(BF16)
    for c in range(KV_COLS // LANES):
        cs = slice(c * LANES, (c + 1) * LANES)
        k_ref[:, cs] = _rope(k[:, cs], cos, sin).astype(BF16)
    v_ref[...] = v.astype(BF16)


def _qkv_layer(x, mods, ng, wqkv, cos, sin):
    def tab(i):
        return (jnp.where(i < ROW_PT, ROW_TPS, (i - ROW_PT) % ROW_TPS), 0, 0)

    def state(i):
        return (jnp.minimum(i, ROW_PT - 1), 0)

    return pl.pallas_call(
        _qkv_kernel,
        out_shape=(jax.ShapeDtypeStruct((NTOK, D), BF16),
                   jax.ShapeDtypeStruct((NTOK, KV_COLS), BF16),
                   jax.ShapeDtypeStruct((NTOK, KV_COLS), BF16),
                   jax.ShapeDtypeStruct((NP_TOK, KV_COLS), F32),
                   jax.ShapeDtypeStruct((NP_TOK, KV_COLS), F32)),
        grid=(NTOK // ROW_T,),
        in_specs=[pl.BlockSpec((ROW_T, D), lambda i: (i, 0)),
                  _mod_spec(1, ROW_T),
                  _const_spec((1, D)),
                  _const_spec((D, QKV_COLS)),
                  pl.BlockSpec((None, ROW_T, LANES), tab),
                  pl.BlockSpec((None, ROW_T, LANES), tab)],
        out_specs=(pl.BlockSpec((ROW_T, D), lambda i: (i, 0)),
                   pl.BlockSpec((ROW_T, KV_COLS), lambda i: (i, 0)),
                   pl.BlockSpec((ROW_T, KV_COLS), lambda i: (i, 0)),
                   pl.BlockSpec((ROW_T, KV_COLS), state),
                   pl.BlockSpec((ROW_T, KV_COLS), state)),
        compiler_params=_cparams(1),
        name="qkv_rope",
    )(x, mods, ng, wqkv, cos, sin)


def _attn_heads(q, parts, sink_ref, o_ref):
    nt = (((1,), (1,)), ((), ()))
    for h in range(N_HEADS):
        kv = h // GQA
        hs = slice(kv * HD, (kv + 1) * HD)
        qh = q[:, h * HD:(h + 1) * HD]
        sink = sink_ref[h]
        scores = []
        m = jnp.full((q.shape[0], 1), sink, F32)
        for k, _, mask in parts:
            s = lax.dot_general(qh, k[:, hs], nt, preferred_element_type=F32)
            if mask is not None:
                s = jnp.where(mask, s, NEG_INF)
            scores.append(s)
            m = jnp.maximum(m, jnp.max(s, axis=-1, keepdims=True))
        denom = jnp.exp(sink - m)
        o = jnp.zeros((q.shape[0], HD), F32)
        for s, (_, v, _) in zip(scores, parts):
            p = jnp.exp(s - m)
            denom = denom + jnp.sum(p, axis=-1, keepdims=True)
            o = o + jnp.dot(p.astype(BF16), v[:, hs], preferred_element_type=F32)
        o_ref[:, h * HD:(h + 1) * HD] = (o / denom).astype(o_ref.dtype)


def _ctx_attn_kernel(sink_ref, q_ref, k_ref, v_ref, o_ref):
    _attn_heads(q_ref[...], [(k_ref[...], v_ref[...], None)], sink_ref, o_ref)


def _lat_attn_kernel(sink_ref, q_ref, k_ref, v_ref, ck_ref, cv_ref, o_ref):
    j = pl.program_id(1)
    w0 = pl.multiple_of(jnp.clip(j * ATT_T - WINDOW, 0, DEC_SEQ - ATT_WIN), WINDOW)
    qpos = j * ATT_T + lax.broadcasted_iota(jnp.int32, (ATT_T, ATT_WIN), 0)
    kpos = w0 + lax.broadcasted_iota(jnp.int32, (ATT_T, ATT_WIN), 1)
    mask = jnp.abs(qpos - kpos) <= WINDOW
    parts = [(k_ref[pl.ds(w0, ATT_WIN), :], v_ref[pl.ds(w0, ATT_WIN), :], mask),
             (ck_ref[...].astype(BF16), cv_ref[...].astype(BF16), None)]
    _attn_heads(q_ref[...], parts, sink_ref, o_ref)


def _attention(q, k, v, ck, cv, sink):
    smem = pl.BlockSpec(memory_space=pltpu.SMEM)
    o_ctx = pl.pallas_call(
        _ctx_attn_kernel,
        out_shape=jax.ShapeDtypeStruct((NP_TOK, D), BF16),
        grid=(BATCH,),
        in_specs=[smem,
                  pl.BlockSpec((SEQ, D), lambda b: (b, 0)),
                  pl.BlockSpec((SEQ, KV_COLS), lambda b: (b, 0)),
                  pl.BlockSpec((SEQ, KV_COLS), lambda b: (b, 0))],
        out_specs=pl.BlockSpec((SEQ, D), lambda b: (b, 0)),
        compiler_params=_cparams(1),
        name="ctx_attention",
    )(sink, q, k, v)

    q_off = NP_TOK // ATT_T
    kv_off = NP_TOK // DEC_SEQ
    tps = DEC_SEQ // ATT_T
    o_lat = pl.pallas_call(
        _lat_attn_kernel,
        out_shape=jax.ShapeDtypeStruct((NS_TOK, D), BF16),
        grid=(DEC_BATCH, tps),
        in_specs=[smem,
                  pl.BlockSpec((ATT_T, D), lambda b, j: (q_off + b * tps + j, 0)),
                  pl.BlockSpec((DEC_SEQ, KV_COLS), lambda b, j: (kv_off + b, 0)),
                  pl.BlockSpec((DEC_SEQ, KV_COLS), lambda b, j: (kv_off + b, 0)),
                  pl.BlockSpec((None, PAST, KV_COLS), lambda b, j: (b, 0, 0)),
                  pl.BlockSpec((None, PAST, KV_COLS), lambda b, j: (b, 0, 0))],
        out_specs=pl.BlockSpec((ATT_T, D), lambda b, j: (b * tps + j, 0)),
        compiler_params=_cparams(2),
        name="latent_attention",
    )(sink, q, k, v, ck, cv)
    return o_ctx, o_lat


def _wo_router_kernel(oc_ref, ol_ref, x_ref, mod_ref, ng_ref, wo_ref, r_ref,
                      x3_ref, h_ref, meta_ref, cnt_ref, carry):
    i = pl.program_id(0)

    @pl.when(i == 0)
    def _():
        carry[...] = jnp.zeros_like(carry)

    a = jnp.where(i < ROW_PT, oc_ref[...], ol_ref[...])
    o = jnp.dot(a, wo_ref[...], preferred_element_type=F32)
    x3 = x_ref[...] + mod_ref[2:3] * o
    x3_ref[...] = x3
    h = _norm_mod(x3, ng_ref[...], mod_ref[3:4], mod_ref[4:5])
    h_ref[...] = h

    logit = [jnp.sum(h * r_ref[e:e + 1, :], axis=-1, keepdims=True) for e in range(N_EXP)]

    def top1(cols):
        m = functools.reduce(jnp.maximum, cols)
        idx = jnp.full_like(m, float(N_EXP))
        for e in reversed(range(N_EXP)):
            idx = jnp.where(cols[e] == m, float(e), idx)
        return m, idx

    m1, e1 = top1(logit)
    m2, e2 = top1([jnp.where(e1 == float(e), -jnp.inf, logit[e]) for e in range(N_EXP)])
    lane = lax.broadcasted_iota(jnp.int32, (ROW_T, LANES), 1).astype(F32)
    t = jnp.exp(m2 - m1)
    g1 = 1.0 / (1.0 + t)
    g2 = t / (1.0 + t)

    oh1 = (lane == e1).astype(F32)
    oh2 = (lane == e2).astype(F32)
    oh = oh1 + oh2
    r_i = lax.broadcasted_iota(jnp.int32, (ROW_T, ROW_T), 0)
    c_i = lax.broadcasted_iota(jnp.int32, (ROW_T, ROW_T), 1)
    tri = (c_i < r_i).astype(BF16)
    before = jnp.dot(tri, oh.astype(BF16), preferred_element_type=F32) + carry[0:1, :]
    rank1 = jnp.sum(before * oh1, axis=-1, keepdims=True)
    rank2 = jnp.sum(before * oh2, axis=-1, keepdims=True)
    carry[0:1, :] = carry[0:1, :] + jnp.sum(oh, axis=0, keepdims=True)
    cnt_ref[...] = carry[...]

    meta = jnp.where(lane == 0, e1, 0.0)
    meta = jnp.where(lane == 1, e2, meta)
    meta = jnp.where(lane == 2, rank1, meta)
    meta = jnp.where(lane == 3, rank2, meta)
    meta = jnp.where(lane == 4, g1, meta)
    meta = jnp.where(lane == 5, g2, meta)
    meta_ref[...] = meta


def _wo_router(o_ctx, o_lat, x, mods, ng, wo, router_t):
    return pl.pallas_call(
        _wo_router_kernel,
        out_shape=(jax.ShapeDtypeStruct((NTOK, D), F32),
                   jax.ShapeDtypeStruct((NTOK, D), F32),
                   jax.ShapeDtypeStruct((NTOK, LANES), F32),
                   jax.ShapeDtypeStruct((8, LANES), F32)),
        grid=(NTOK // ROW_T,),
        in_specs=[pl.BlockSpec((ROW_T, D), lambda i: (jnp.minimum(i, ROW_PT - 1), 0)),
                  pl.BlockSpec((ROW_T, D), lambda i: (jnp.maximum(i - ROW_PT, 0), 0)),
                  pl.BlockSpec((ROW_T, D), lambda i: (i, 0)),
                  _mod_spec(1, ROW_T),
                  _const_spec((1, D)),
                  _const_spec((D, D)),
                  _const_spec((N_EXP, D))],
        out_specs=(pl.BlockSpec((ROW_T, D), lambda i: (i, 0)),
                   pl.BlockSpec((ROW_T, D), lambda i: (i, 0)),
                   pl.BlockSpec((ROW_T, LANES), lambda i: (i, 0)),
                   pl.BlockSpec((8, LANES), lambda i: (0, 0))),
        scratch_shapes=[pltpu.VMEM((8, LANES), F32)],
        compiler_params=_cparams(1),
        name="wo_router",
    )(o_ctx, o_lat, x, mods, ng, wo, router_t)


MOE_NF = D_FFE // MOE_FT
assert MOE_NF == 2
MOE_STEPS = MOE_TILES + 1
Y_IDLE = 2 * NTOK + N_EXP * MOE_T
Y_ROWS = Y_IDLE + MOE_T
MAP_ROWS = (MOE_TILES + 2) * MOE_T
assert NTOK & (NTOK - 1) == 0


def _expert_kernel(te_ref, tv_ref, map_ref, h_ref, wu_ref, wg_ref, w2_ref, y_ref,
                   xbuf, xb, obuf, gsem, ssem):
    j = pl.program_id(0)
    f = pl.program_id(1)
    cur = j % 2
    nxt = 1 - cur
    computes = tv_ref[j + 1] == 1
    drains = jnp.logical_and(tv_ref[j] == 1, jnp.logical_not(computes))

    def gather(tile, slot, r):
        src = map_ref[(tile + 1) * MOE_T + r] & (NTOK - 1)
        return pltpu.make_async_copy(h_ref.at[pl.ds(src, 1)], xbuf.at[slot, pl.ds(r, 1)],
                                     gsem.at[slot])

    def gather_all(slot):
        return pltpu.make_async_copy(h_ref.at[pl.ds(0, MOE_T)], xbuf.at[slot], gsem.at[slot])

    def scatter(tile, slot, r):
        dst = map_ref[(tile + 1) * MOE_T + r]
        return pltpu.make_async_copy(obuf.at[slot, pl.ds(r, 1)], y_ref.at[pl.ds(dst, 1)],
                                     ssem.at[slot])

    def scatter_all(slot):
        return pltpu.make_async_copy(obuf.at[slot], y_ref.at[pl.ds(0, MOE_T)], ssem.at[slot])

    @pl.when(jnp.logical_and(j == 0, f == 0))
    def _():
        obuf[...] = jnp.zeros_like(obuf)
        for s in range(N_EXP):
            spare = pltpu.make_async_copy(
                obuf.at[0], y_ref.at[pl.ds(2 * NTOK + s * MOE_T, MOE_T)], ssem.at[0])
            spare.start()
            spare.wait()

        def first(r, c):
            gather(0, 0, r).start()
            scatter(-1, 0, r).start()
            return c

        lax.fori_loop(0, MOE_T, first, 0)

    @pl.when(jnp.logical_and(computes, f == 0))
    def _():
        gather_all(cur).wait()
        xb[...] = xbuf[cur].astype(BF16)
        for r in range(MOE_T):
            gather(j + 1, nxt, r).start()
        x = xb[...]
        u = jnp.dot(x, wu_ref[...], preferred_element_type=F32)
        g = jnp.dot(x, wg_ref[...], preferred_element_type=F32)
        a = (_silu(u) * g).astype(BF16)
        y = jnp.dot(a, w2_ref[...], preferred_element_type=F32)
        scatter_all(cur).wait()
        obuf[cur] = y

    @pl.when(jnp.logical_and(computes, f == 1))
    def _():
        for r in range(MOE_T):
            scatter(j - 1, nxt, r).start()
        x = xb[...]
        u = jnp.dot(x, wu_ref[...], preferred_element_type=F32)
        g = jnp.dot(x, wg_ref[...], preferred_element_type=F32)
        a = (_silu(u) * g).astype(BF16)
        y = jnp.dot(a, w2_ref[...], preferred_element_type=F32)
        obuf[cur] = obuf[cur] + y

    @pl.when(jnp.logical_and(drains, f == 0))
    def _():
        gather_all(cur).wait()
        scatter_all(cur).wait()

    @pl.when(jnp.logical_and(drains, f == 1))
    def _():
        def last(r, c):
            scatter(j - 1, nxt, r).start()
            return c

        lax.fori_loop(0, MOE_T, last, 0)
        scatter_all(nxt).wait()


def _experts(tile_e, tile_v, rowmap, h, w13, w2):
    def fidx(j, f, tv):
        return jnp.where(tv[j + 1] == 1, f, MOE_NF - 1)

    return pl.pallas_call(
        _expert_kernel,
        out_shape=jax.ShapeDtypeStruct((Y_ROWS, D), F32),
        grid_spec=pltpu.PrefetchScalarGridSpec(
            num_scalar_prefetch=3,
            grid=(MOE_STEPS, MOE_NF),
            in_specs=[
                pl.BlockSpec(memory_space=pl.ANY),
                pl.BlockSpec((None, D, MOE_FT), lambda j, f, te, tv, *_: (te[j], 0, fidx(j, f, tv))),
                pl.BlockSpec((None, D, MOE_FT),
                             lambda j, f, te, tv, *_: (te[j], 0, MOE_NF + fidx(j, f, tv))),
                pl.BlockSpec((None, MOE_FT, D), lambda j, f, te, tv, *_: (te[j], fidx(j, f, tv), 0)),
            ],
            out_specs=pl.BlockSpec(memory_space=pl.ANY),
            scratch_shapes=[pltpu.VMEM((2, MOE_T, D), F32), pltpu.VMEM((MOE_T, D), BF16),
                            pltpu.VMEM((2, MOE_T, D), F32),
                            pltpu.SemaphoreType.DMA((2,)), pltpu.SemaphoreType.DMA((2,))]),
        compiler_params=_cparams(2),
        name="moe_experts",
    )(tile_e, tile_v, rowmap, h, w13, w13, w2)


def _rowmap_kernel(lo_ref, hi_ref, pos_ref, map_ref):
    def fill(start, stop, value_at):
        def body(i, c):
            map_ref[i] = value_at(i)
            return c
        lax.fori_loop(start, stop, body, 0)

    fill(0, MOE_T, lambda i: Y_IDLE + i)
    for e in range(N_EXP):
        lo = lo_ref[e]
        fill(MOE_T + lo, MOE_T + hi_ref[e], lambda i: 2 * NTOK + e * MOE_T + (i - MOE_T - lo))
    fill(MOE_T + hi_ref[N_EXP - 1], MAP_ROWS, lambda i: Y_IDLE + (i & (MOE_T - 1)))

    def token(t, c):
        map_ref[MOE_T + pos_ref[t]] = t
        map_ref[MOE_T + pos_ref[NTOK + t]] = NTOK + t
        return c

    lax.fori_loop(0, NTOK, token, 0, unroll=8)


def _rowmap(pad_lo, pad_hi, pos):
    smem = pl.BlockSpec(memory_space=pltpu.SMEM)
    return pl.pallas_call(
        _rowmap_kernel,
        out_shape=jax.ShapeDtypeStruct((MAP_ROWS,), jnp.int32),
        in_specs=[smem, smem, smem],
        out_specs=smem,
        name="moe_rowmap",
    )(pad_lo, pad_hi, pos)


def _combine_kernel(y1_ref, y2_ref, meta_ref, x_ref, mod_ref, fg_ref, yp_ref, ys_ref):
    i = pl.program_id(0)
    meta = meta_ref[...]
    y = meta[:, 4:5] * y1_ref[...] + meta[:, 5:6] * y2_ref[...]
    x = x_ref[...] + mod_ref[5:6] * y
    out = x * lax.rsqrt(jnp.mean(x * x, axis=-1, keepdims=True) + EPS) * fg_ref[...]

    @pl.when(i < ROW_PT)
    def _():
        yp_ref[...] = out

    @pl.when(i >= ROW_PT)
    def _():
        ys_ref[...] = out


def _combine(y, meta, x, mods, final_g):
    slot2 = NTOK // ROW_T
    return pl.pallas_call(
        _combine_kernel,
        out_shape=(jax.ShapeDtypeStruct((NP_TOK, D), F32),
                   jax.ShapeDtypeStruct((NS_TOK, D), F32)),
        grid=(NTOK // ROW_T,),
        in_specs=[pl.BlockSpec((ROW_T, D), lambda i: (i, 0)),
                  pl.BlockSpec((ROW_T, D), lambda i: (slot2 + i, 0)),
                  pl.BlockSpec((ROW_T, LANES), lambda i: (i, 0)),
                  pl.BlockSpec((ROW_T, D), lambda i: (i, 0)),
                  _mod_spec(1, ROW_T),
                  _const_spec((1, D))],
        out_specs=(pl.BlockSpec((ROW_T, D), lambda i: (jnp.minimum(i, ROW_PT - 1), 0)),
                   pl.BlockSpec((ROW_T, D), lambda i: (jnp.maximum(i - ROW_PT, 0), 0))),
        compiler_params=_cparams(1),
        name="moe_combine",
    )(y, y, meta, x, mods, final_g)


def _routing_tables(meta, counts):
    cnt = counts[0, :N_EXP].astype(jnp.int32)
    padded = (cnt + MOE_T - 1) // MOE_T * MOE_T
    ends = jnp.cumsum(padded)
    off = ends - padded
    experts = jnp.arange(N_EXP, dtype=jnp.int32)

    def group_start(e):
        return jnp.sum(jnp.where(e[:, None] == experts[None, :], off[None, :], 0), axis=1)

    pos = jnp.concatenate([group_start(meta[:, 0].astype(jnp.int32)) + meta[:, 2].astype(jnp.int32),
                           group_start(meta[:, 1].astype(jnp.int32)) + meta[:, 3].astype(jnp.int32)])
    n_valid = ends[-1] // MOE_T
    tile = jnp.arange(MOE_STEPS, dtype=jnp.int32)
    first_row = jnp.minimum(tile, n_valid - 1) * MOE_T
    tile_e = jnp.sum((ends[None, :] <= first_row[:, None]).astype(jnp.int32), axis=1)
    tile_e = jnp.minimum(tile_e, N_EXP - 1)
    tile_v = jnp.concatenate([jnp.zeros((1,), jnp.int32), (tile < n_valid).astype(jnp.int32)])
    return tile_e, tile_v, off + cnt, ends, pos


def kernel(x_prompt, x_sample, cache_k_l1, cache_v_l1, c, c_ctx, ada_w, ada_b, norm_g, conv_w1, conv_b1, conv_dw, conv_dwb, conv_ln_g, conv_ln_b, conv_w2, attn_wqkv, attn_wo, attn_sink, ffn_w13, ffn_w2, moe_router, moe_w13, moe_w2, final_g):
    xp = x_prompt.reshape(NP_TOK, D)
    xs = x_sample.reshape(NS_TOK, D)
    cond = jnp.concatenate([c, c_ctx[None, :], jnp.zeros((N_COND - DEC_BATCH - 1, D), F32)])
    mods = _ada(cond, ada_w, ada_b)

    row = lambda a: a.reshape(1, -1)
    x1 = _conv_layer(xp, xs, mods, row(norm_g[0, 0]), conv_w1.astype(BF16), row(conv_b1),
                     conv_dw, row(conv_dwb), row(conv_ln_g), row(conv_ln_b), conv_w2.astype(BF16))
    x2 = _ffn_layer(x1, mods, row(norm_g[0, 1]), ffn_w13.astype(BF16), ffn_w2.astype(BF16))

    cos, sin = _rope_tables()
    q, k, v, state_k, state_v = _qkv_layer(x2, mods, row(norm_g[1, 0]), attn_wqkv.astype(BF16),
                                           cos, sin)
    o_ctx, o_lat = _attention(q, k, v, cache_k_l1.reshape(DEC_BATCH, PAST, KV_COLS),
                              cache_v_l1.reshape(DEC_BATCH, PAST, KV_COLS), attn_sink)

    x3, h, meta, counts = _wo_router(o_ctx, o_lat, x2, mods, row(norm_g[1, 1]),
                                     attn_wo.astype(BF16), moe_router.T)
    tile_e, tile_v, pad_lo, pad_hi, pos = _routing_tables(meta, counts)
    rowmap = _rowmap(pad_lo, pad_hi, pos)
    y = _experts(tile_e, tile_v, rowmap, h, moe_w13.astype(BF16), moe_w2.astype(BF16))
    y_p, y_s = _combine(y, meta, x3, mods, row(final_g))

    return (y_p.reshape(BATCH, SEQ, D), y_s.reshape(DEC_BATCH, DEC_SEQ, D),
            state_k.reshape(BATCH, SEQ, N_KV, HD), state_v.reshape(BATCH, SEQ, N_KV, HD))
```

```python
import functools

import jax
import jax.numpy as jnp
from jax import lax
from jax.experimental import pallas as pl
from jax.experimental.pallas import tpu as pltpu

F32 = jnp.float32
BF16 = jnp.bfloat16

D = 1024
BATCH, SEQ = 32, 256
DEC_BATCH, DEC_SEQ = 4, 2048
PAST = 256
NP_TOK = BATCH * SEQ
NS_TOK = DEC_BATCH * DEC_SEQ
NTOK = NP_TOK + NS_TOK
GRID_W = 64
N_HEADS, N_KV, HD = 16, 4, 64
GQA = N_HEADS // N_KV
KV_COLS = N_KV * HD
QKV_COLS = D + 2 * KV_COLS
WINDOW = 128
ROPE_THETA = 10000.0
CONV_W = 31
CONV_PAD = CONV_W // 2
D_FF = 2816
N_EXP = 8
D_FFE = 3584
N_MOD = 6
EPS = 1e-6
NEG_INF = -1e30

N_COND = 8
CTX_ROW = DEC_BATCH

CONV_T = 256
HALO = 16
ROW_T = 512
ATT_T = 256
ATT_WIN = ATT_T + 2 * WINDOW
FF_T = 1408
MOE_T = 512
MOE_FT = 1792
MOE_TILES = 2 * NTOK // MOE_T + N_EXP
MOE_ROWS = MOE_TILES * MOE_T
LANES = 128

VMEM_LIMIT = 56 * 1024 * 1024


def _cparams(n_axes=1, vmem=VMEM_LIMIT):
    return pltpu.CompilerParams(dimension_semantics=("arbitrary",) * n_axes,
                                vmem_limit_bytes=vmem)


def _silu(x):
    return x * jax.nn.sigmoid(x)


def _norm_mod(x, g, shift, scale):
    y = x * lax.rsqrt(jnp.mean(x * x, axis=-1, keepdims=True) + EPS)
    return (y * g) * (1.0 + scale) + shift


def _cond_row(tile, tile_rows):
    n_prompt_tiles = NP_TOK // tile_rows
    per_seq = DEC_SEQ // tile_rows
    return jnp.where(tile < n_prompt_tiles, CTX_ROW, (tile - n_prompt_tiles) // per_seq)


def _mod_spec(layer, tile_rows):
    return pl.BlockSpec((None, None, N_MOD, D),
                        lambda i, *_: (layer, _cond_row(i, tile_rows), 0, 0))


def _const_spec(shape):
    nd = len(shape)
    return pl.BlockSpec(shape, lambda *_: (0,) * nd)


BF16_SUBLANES = 16


def _cast_specs(arrays, steps):
    specs, shapes = [], []
    for a in arrays:
        rows, cols = a.shape
        n = steps
        while rows % n or (rows // n) % BF16_SUBLANES:
            n -= 1
        specs.append(pl.BlockSpec((rows // n, cols), lambda i, n=n: (jnp.minimum(i, n - 1), 0)))
        shapes.append(jax.ShapeDtypeStruct((rows, cols), BF16))
    return specs, shapes


def _cast_blocks(in_refs, out_refs):
    for w_ref, o_ref in zip(in_refs, out_refs):
        o_ref[...] = w_ref[...].astype(BF16)


ADA_TN = 1536


def _ada_kernel(cond_ref, w_ref, b_ref, o_ref):
    s = _silu(cond_ref[...]).astype(BF16)
    o_ref[...] = jnp.dot(s, w_ref[...].astype(BF16), preferred_element_type=F32) + b_ref[...]


def _ada(cond, ada_w, ada_b):
    depth = ada_w.shape[0]
    out = pl.pallas_call(
        _ada_kernel,
        out_shape=jax.ShapeDtypeStruct((depth, N_COND, N_MOD * D), F32),
        grid=(depth, N_MOD * D // ADA_TN),
        in_specs=[pl.BlockSpec((N_COND, D), lambda l, n: (0, 0)),
                  pl.BlockSpec((None, D, ADA_TN), lambda l, n: (l, 0, n)),
                  pl.BlockSpec((None, 1, ADA_TN), lambda l, n: (l, 0, n))],
        out_specs=pl.BlockSpec((None, N_COND, ADA_TN), lambda l, n: (l, 0, n)),
        compiler_params=_cparams(2),
        name="adaln",
    )(cond, ada_w, ada_b.reshape(depth, 1, N_MOD * D))
    return out.reshape(depth, N_COND, N_MOD, D)


CONV_PT = NP_TOK // CONV_T
CONV_TPS = DEC_SEQ // CONV_T
CONV_E = CONV_T + 2 * HALO


CONV_NIN = 13


def _conv_kernel(*refs, n_cast):
    (xp_ref, xs_ref, xprev_ref, xnext_ref, mod_ref, ng_ref, w1_ref, b1_ref,
     dw_ref, dwb_ref, lng_ref, lnb_ref, w2_ref) = refs[:CONV_NIN]
    o_ref = refs[CONV_NIN + n_cast]
    g_scr, y_scr = refs[-2:]
    _cast_blocks(refs[CONV_NIN:CONV_NIN + n_cast], refs[CONV_NIN + n_cast + 1:-2])
    i = pl.program_id(0)
    j = i - CONV_PT
    is_prompt = i < CONV_PT
    has_prev = jnp.logical_and(i >= CONV_PT, j % CONV_TPS != 0)
    has_next = jnp.logical_and(i >= CONV_PT, j % CONV_TPS != CONV_TPS - 1)
    x_main = jnp.where(is_prompt, xp_ref[...], xs_ref[...])
    shift, scale, gate = mod_ref[0:1], mod_ref[1:2], mod_ref[2:3]

    xe = jnp.concatenate([xprev_ref[...], x_main, xnext_ref[...]], axis=0)
    h = _norm_mod(xe, ng_ref[...], shift, scale).astype(BF16)
    uv = jnp.dot(h, w1_ref[...], preferred_element_type=F32) + b1_ref[...]
    g = uv[:, :D] * jax.nn.sigmoid(uv[:, D:])
    g_scr[0:HALO] = jnp.where(has_prev, g[0:HALO], 0.0)
    g_scr[HALO:HALO + CONV_T] = g[HALO:HALO + CONV_T]
    g_scr[HALO + CONV_T:] = jnp.where(has_next, g[HALO + CONV_T:], 0.0)

    base = HALO - CONV_PAD
    sub = 8
    for c in range(D // LANES):
        cs = slice(c * LANES, (c + 1) * LANES)
        y = dwb_ref[:, cs]
        for r in range(sub):
            z = None
            for a in range((base + CONV_W - 1) // sub + 1):
                k = sub * a + r - base
                if 0 <= k < CONV_W:
                    term = g_scr[sub * a:sub * a + CONV_T + sub, cs] * dw_ref[k:k + 1, cs]
                    z = term if z is None else z + term
            y = y + z[r:r + CONV_T]
        y_scr[:, cs] = y

    y = y_scr[...]
    mu = jnp.mean(y, axis=-1, keepdims=True)
    d = y - mu
    var = jnp.mean(d * d, axis=-1, keepdims=True)
    yn = d * lax.rsqrt(var + EPS) * lng_ref[...] + lnb_ref[...]
    o = jnp.dot(_silu(yn).astype(BF16), w2_ref[...], preferred_element_type=F32)
    o_ref[...] = x_main + gate * o


def _conv_layer(xp, xs, mods, ng, w1, b1, dw, dwb, lng, lnb, w2, to_cast):
    n_tiles = NTOK // CONV_T
    hb = CONV_T // HALO
    n_hb = NS_TOK // HALO
    cast_specs, cast_shapes = _cast_specs(to_cast, n_tiles)

    def s_tile(i):
        return jnp.maximum(i - CONV_PT, 0)

    outs = pl.pallas_call(
        functools.partial(_conv_kernel, n_cast=len(to_cast)),
        out_shape=[jax.ShapeDtypeStruct((NTOK, D), F32)] + cast_shapes,
        grid=(n_tiles,),
        in_specs=[
            pl.BlockSpec((CONV_T, D), lambda i: (jnp.minimum(i, CONV_PT - 1), 0)),
            pl.BlockSpec((CONV_T, D), lambda i: (s_tile(i), 0)),
            pl.BlockSpec((HALO, D), lambda i: (jnp.maximum(s_tile(i) * hb - 1, 0), 0)),
            pl.BlockSpec((HALO, D), lambda i: (jnp.minimum(s_tile(i) * hb + hb, n_hb - 1), 0)),
            _mod_spec(0, CONV_T),
            _const_spec((1, D)),
            _const_spec((D, 2 * D)),
            _const_spec((1, 2 * D)),
            _const_spec((CONV_W, D)),
            _const_spec((1, D)),
            _const_spec((1, D)),
            _const_spec((1, D)),
            _const_spec((D, D)),
        ] + cast_specs,
        out_specs=[pl.BlockSpec((CONV_T, D), lambda i: (i, 0))] + cast_specs,
        scratch_shapes=[pltpu.VMEM((CONV_E, D), F32), pltpu.VMEM((CONV_T, D), F32)],
        compiler_params=_cparams(1),
        name="conv_mixer",
    )(xp, xs, xs, xs, mods, ng, w1, b1, dw, dwb, lng, lnb, w2, *to_cast)
    return outs[0], outs[1:]


def _ffn_kernel(x_ref, mod_ref, ng_ref, w13_ref, w2_ref, cast_ref, o_ref, cast_out_ref):
    _cast_blocks([cast_ref], [cast_out_ref])
    x = x_ref[...]
    h = _norm_mod(x, ng_ref[...], mod_ref[3:4], mod_ref[4:5]).astype(BF16)
    acc = jnp.zeros((ROW_T, D), F32)
    for f in range(D_FF // FF_T):
        u = jnp.dot(h, w13_ref[:, f * FF_T:(f + 1) * FF_T], preferred_element_type=F32)
        g = jnp.dot(h, w13_ref[:, D_FF + f * FF_T:D_FF + (f + 1) * FF_T],
                    preferred_element_type=F32)
        a = (_silu(u) * g).astype(BF16)
        acc = acc + jnp.dot(a, w2_ref[f * FF_T:(f + 1) * FF_T, :], preferred_element_type=F32)
    o_ref[...] = x + mod_ref[5:6] * acc


def _ffn_layer(x, mods, ng, w13, w2, to_cast):
    steps = NTOK // ROW_T
    cast_specs, cast_shapes = _cast_specs([to_cast], steps)
    return pl.pallas_call(
        _ffn_kernel,
        out_shape=[jax.ShapeDtypeStruct((NTOK, D), F32)] + cast_shapes,
        grid=(steps,),
        in_specs=[pl.BlockSpec((ROW_T, D), lambda i: (i, 0)),
                  _mod_spec(0, ROW_T),
                  _const_spec((1, D)),
                  pl.BlockSpec((D, 2 * D_FF), lambda i: (0, 0), pipeline_mode=pl.Buffered(1)),
                  pl.BlockSpec((D_FF, D), lambda i: (0, 0), pipeline_mode=pl.Buffered(1))]
        + cast_specs,
        out_specs=[pl.BlockSpec((ROW_T, D), lambda i: (i, 0))] + cast_specs,
        compiler_params=_cparams(1),
        name="dense_ffn",
    )(x, mods, ng, w13, w2, to_cast)


ROW_PT = NP_TOK // ROW_T
ROW_TPS = DEC_SEQ // ROW_T


def _rope_tables():
    half = HD // 2
    inv = ROPE_THETA ** (-jnp.arange(0, half, 2, dtype=F32) / half)
    t = jnp.arange(DEC_SEQ)
    lane = jnp.arange(LANES)
    jj = lane % HD
    pos = jnp.where(jj[None, :] < half, (t // GRID_W)[:, None], (t % GRID_W)[:, None]).astype(F32)
    ang = pos * inv[jj % (half // 2)][None, :]
    first = (jj % half) < (half // 2)
    cos = jnp.cos(ang)
    sin = jnp.where(first[None, :], -jnp.sin(ang), jnp.sin(ang))
    cos = jnp.concatenate([cos.reshape(ROW_TPS, ROW_T, LANES), jnp.ones((1, ROW_T, LANES), F32)])
    sin = jnp.concatenate([sin.reshape(ROW_TPS, ROW_T, LANES), jnp.zeros((1, ROW_T, LANES), F32)])
    return cos, sin


def _rope(x, cos, sin):
    lane = lax.broadcasted_iota(jnp.int32, x.shape, 1)
    first = (lane % (HD // 2)) < (HD // 4)
    swapped = jnp.where(first, pltpu.roll(x, LANES - HD // 4, axis=1), pltpu.roll(x, HD // 4, axis=1))
    return x * cos + swapped * sin


def _qkv_kernel(x_ref, mod_ref, ng_ref, w_ref, cos_ref, sin_ref,
                q_ref, k_ref, v_ref, sk_ref, sv_ref):
    i = pl.program_id(0)
    h = _norm_mod(x_ref[...], ng_ref[...], mod_ref[0:1], mod_ref[1:2]).astype(BF16)
    y = jnp.dot(h, w_ref[...], preferred_element_type=F32)
    k = y[:, D:D + KV_COLS]
    v = y[:, D + KV_COLS:]

    @pl.when(i < ROW_PT)
    def _():
        sk_ref[...] = k
        sv_ref[...] = v

    cos, sin = cos_ref[...], sin_ref[...]
    scale = HD ** -0.5
    for c in range(D // LANES):
        cs = slice(c * LANES, (c + 1) * LANES)
        q_ref[:, cs] = (_rope(y[:, cs], cos, sin) * scale).astype(BF16)
    for c in range(KV_COLS // LANES):
        cs = slice(c * LANES, (c + 1) * LANES)
        k_ref[:, cs] = _rope(k[:, cs], cos, sin).astype(BF16)
    v_ref[...] = v.astype(BF16)


def _qkv_layer(x, mods, ng, wqkv, cos, sin):
    def tab(i):
        return (jnp.where(i < ROW_PT, ROW_TPS, (i - ROW_PT) % ROW_TPS), 0, 0)

    def state(i):
        return (jnp.minimum(i, ROW_PT - 1), 0)

    return pl.pallas_call(
        _qkv_kernel,
        out_shape=(jax.ShapeDtypeStruct((NTOK, D), BF16),
                   jax.ShapeDtypeStruct((NTOK, KV_COLS), BF16),
                   jax.ShapeDtypeStruct((NTOK, KV_COLS), BF16),
                   jax.ShapeDtypeStruct((NP_TOK, KV_COLS), F32),
                   jax.ShapeDtypeStruct((NP_TOK, KV_COLS), F32)),
        grid=(NTOK // ROW_T,),
        in_specs=[pl.BlockSpec((ROW_T, D), lambda i: (i, 0)),
                  _mod_spec(1, ROW_T),
                  _const_spec((1, D)),
                  _const_spec((D, QKV_COLS)),
                  pl.BlockSpec((None, ROW_T, LANES), tab),
                  pl.BlockSpec((None, ROW_T, LANES), tab)],
        out_specs=(pl.BlockSpec((ROW_T, D), lambda i: (i, 0)),
                   pl.BlockSpec((ROW_T, KV_COLS), lambda i: (i, 0)),
                   pl.BlockSpec((ROW_T, KV_COLS), lambda i: (i, 0)),
                   pl.BlockSpec((ROW_T, KV_COLS), state),
                   pl.BlockSpec((ROW_T, KV_COLS), state)),
        compiler_params=_cparams(1),
        name="qkv_rope",
    )(x, mods, ng, wqkv, cos, sin)


def _attn_heads(q, parts, sink_ref, o_ref):
    nt = (((1,), (1,)), ((), ()))
    for h in range(N_HEADS):
        kv = h // GQA
        hs = slice(kv * HD, (kv + 1) * HD)
        qh = q[:, h * HD:(h + 1) * HD]
        sink = sink_ref[h]
        scores = []
        m = jnp.full((q.shape[0], 1), sink, F32)
        for k, _, mask in parts:
            s = lax.dot_general(qh, k[:, hs], nt, preferred_element_type=F32)
            if mask is not None:
                s = jnp.where(mask, s, NEG_INF)
            scores.append(s)
            m = jnp.maximum(m, jnp.max(s, axis=-1, keepdims=True))
        denom = jnp.exp(sink - m)
        o = jnp.zeros((q.shape[0], HD), F32)
        for s, (_, v, _) in zip(scores, parts):
            p = jnp.exp(s - m)
            denom = denom + jnp.sum(p, axis=-1, keepdims=True)
            o = o + jnp.dot(p.astype(BF16), v[:, hs], preferred_element_type=F32)
        o_ref[:, h * HD:(h + 1) * HD] = (o / denom).astype(o_ref.dtype)


def _ctx_attn_kernel(sink_ref, q_ref, k_ref, v_ref, o_ref):
    _attn_heads(q_ref[...], [(k_ref[...], v_ref[...], None)], sink_ref, o_ref)


def _lat_attn_kernel(sink_ref, q_ref, k_ref, v_ref, ck_ref, cv_ref, o_ref):
    j = pl.program_id(1)
    w0 = pl.multiple_of(jnp.clip(j * ATT_T - WINDOW, 0, DEC_SEQ - ATT_WIN), WINDOW)
    qpos = j * ATT_T + lax.broadcasted_iota(jnp.int32, (ATT_T, ATT_WIN), 0)
    kpos = w0 + lax.broadcasted_iota(jnp.int32, (ATT_T, ATT_WIN), 1)
    mask = jnp.abs(qpos - kpos) <= WINDOW
    parts = [(k_ref[pl.ds(w0, ATT_WIN), :], v_ref[pl.ds(w0, ATT_WIN), :], mask),
             (ck_ref[...].astype(BF16), cv_ref[...].astype(BF16), None)]
    _attn_heads(q_ref[...], parts, sink_ref, o_ref)


def _attention(q, k, v, ck, cv, sink):
    smem = pl.BlockSpec(memory_space=pltpu.SMEM)
    o_ctx = pl.pallas_call(
        _ctx_attn_kernel,
        out_shape=jax.ShapeDtypeStruct((NP_TOK, D), BF16),
        grid=(BATCH,),
        in_specs=[smem,
                  pl.BlockSpec((SEQ, D), lambda b: (b, 0)),
                  pl.BlockSpec((SEQ, KV_COLS), lambda b: (b, 0)),
                  pl.BlockSpec((SEQ, KV_COLS), lambda b: (b, 0))],
        out_specs=pl.BlockSpec((SEQ, D), lambda b: (b, 0)),
        compiler_params=_cparams(1),
        name="ctx_attention",
    )(sink, q, k, v)

    q_off = NP_TOK // ATT_T
    kv_off = NP_TOK // DEC_SEQ
    tps = DEC_SEQ // ATT_T
    o_lat = pl.pallas_call(
        _lat_attn_kernel,
        out_shape=jax.ShapeDtypeStruct((NS_TOK, D), BF16),
        grid=(DEC_BATCH, tps),
        in_specs=[smem,
                  pl.BlockSpec((ATT_T, D), lambda b, j: (q_off + b * tps + j, 0)),
                  pl.BlockSpec((DEC_SEQ, KV_COLS), lambda b, j: (kv_off + b, 0)),
                  pl.BlockSpec((DEC_SEQ, KV_COLS), lambda b, j: (kv_off + b, 0)),
                  pl.BlockSpec((None, PAST, KV_COLS), lambda b, j: (b, 0, 0)),
                  pl.BlockSpec((None, PAST, KV_COLS), lambda b, j: (b, 0, 0))],
        out_specs=pl.BlockSpec((ATT_T, D), lambda b, j: (b * tps + j, 0)),
        compiler_params=_cparams(2),
        name="latent_attention",
    )(sink, q, k, v, ck, cv)
    return o_ctx, o_lat


def _wo_router_kernel(oc_ref, ol_ref, x_ref, mod_ref, ng_ref, wo_ref, r_ref,
                      x3_ref, h_ref, meta_ref, cnt_ref, carry):
    i = pl.program_id(0)

    @pl.when(i == 0)
    def _():
        carry[...] = jnp.zeros_like(carry)

    a = jnp.where(i < ROW_PT, oc_ref[...], ol_ref[...])
    o = jnp.dot(a, wo_ref[...], preferred_element_type=F32)
    x3 = x_ref[...] + mod_ref[2:3] * o
    x3_ref[...] = x3
    h = _norm_mod(x3, ng_ref[...], mod_ref[3:4], mod_ref[4:5])
    h_ref[...] = h

    logit = [jnp.sum(h * r_ref[e:e + 1, :], axis=-1, keepdims=True) for e in range(N_EXP)]

    def top1(cols):
        m = functools.reduce(jnp.maximum, cols)
        idx = jnp.full_like(m, float(N_EXP))
        for e in reversed(range(N_EXP)):
            idx = jnp.where(cols[e] == m, float(e), idx)
        return m, idx

    m1, e1 = top1(logit)
    m2, e2 = top1([jnp.where(e1 == float(e), -jnp.inf, logit[e]) for e in range(N_EXP)])
    lane = lax.broadcasted_iota(jnp.int32, (ROW_T, LANES), 1).astype(F32)
    t = jnp.exp(m2 - m1)
    g1 = 1.0 / (1.0 + t)
    g2 = t / (1.0 + t)

    oh1 = (lane == e1).astype(F32)
    oh2 = (lane == e2).astype(F32)
    oh = oh1 + oh2
    r_i = lax.broadcasted_iota(jnp.int32, (ROW_T, ROW_T), 0)
    c_i = lax.broadcasted_iota(jnp.int32, (ROW_T, ROW_T), 1)
    tri = (c_i < r_i).astype(BF16)
    before = jnp.dot(tri, oh.astype(BF16), preferred_element_type=F32) + carry[0:1, :]
    rank1 = jnp.sum(before * oh1, axis=-1, keepdims=True)
    rank2 = jnp.sum(before * oh2, axis=-1, keepdims=True)
    carry[0:1, :] = carry[0:1, :] + jnp.sum(oh, axis=0, keepdims=True)
    cnt_ref[...] = carry[...]

    meta = jnp.where(lane == 0, e1, 0.0)
    meta = jnp.where(lane == 1, e2, meta)
    meta = jnp.where(lane == 2, rank1, meta)
    meta = jnp.where(lane == 3, rank2, meta)
    meta = jnp.where(lane == 4, g1, meta)
    meta = jnp.where(lane == 5, g2, meta)
    meta_ref[...] = meta


def _wo_router(o_ctx, o_lat, x, mods, ng, wo, router_t):
    return pl.pallas_call(
        _wo_router_kernel,
        out_shape=(jax.ShapeDtypeStruct((NTOK, D), F32),
                   jax.ShapeDtypeStruct((NTOK, D), F32),
                   jax.ShapeDtypeStruct((NTOK, LANES), F32),
                   jax.ShapeDtypeStruct((8, LANES), F32)),
        grid=(NTOK // ROW_T,),
        in_specs=[pl.BlockSpec((ROW_T, D), lambda i: (jnp.minimum(i, ROW_PT - 1), 0)),
                  pl.BlockSpec((ROW_T, D), lambda i: (jnp.maximum(i - ROW_PT, 0), 0)),
                  pl.BlockSpec((ROW_T, D), lambda i: (i, 0)),
                  _mod_spec(1, ROW_T),
                  _const_spec((1, D)),
                  _const_spec((D, D)),
                  _const_spec((N_EXP, D))],
        out_specs=(pl.BlockSpec((ROW_T, D), lambda i: (i, 0)),
                   pl.BlockSpec((ROW_T, D), lambda i: (i, 0)),
                   pl.BlockSpec((ROW_T, LANES), lambda i: (i, 0)),
                   pl.BlockSpec((8, LANES), lambda i: (0, 0))),
        scratch_shapes=[pltpu.VMEM((8, LANES), F32)],
        compiler_params=_cparams(1),
        name="wo_router",
    )(o_ctx, o_lat, x, mods, ng, wo, router_t)


MOE_NF = D_FFE // MOE_FT
assert MOE_NF == 2
MOE_STEPS = MOE_TILES + 1
Y_IDLE = 2 * NTOK + N_EXP * MOE_T
Y_ROWS = Y_IDLE + MOE_T
MAP_ROWS = (MOE_TILES + 2) * MOE_T
assert NTOK & (NTOK - 1) == 0


def _expert_kernel(te_ref, tv_ref, map_ref, h_ref, wu_ref, wg_ref, w2_ref, y_ref,
                   xbuf, xb, obuf, gsem, ssem):
    j = pl.program_id(0)
    f = pl.program_id(1)
    cur = j % 2
    nxt = 1 - cur
    computes = tv_ref[j + 1] == 1
    drains = jnp.logical_and(tv_ref[j] == 1, jnp.logical_not(computes))

    def gather(tile, slot, r):
        src = map_ref[(tile + 1) * MOE_T + r] & (NTOK - 1)
        return pltpu.make_async_copy(h_ref.at[pl.ds(src, 1)], xbuf.at[slot, pl.ds(r, 1)],
                                     gsem.at[slot])

    def gather_all(slot):
        return pltpu.make_async_copy(h_ref.at[pl.ds(0, MOE_T)], xbuf.at[slot], gsem.at[slot])

    def scatter(tile, slot, r):
        dst = map_ref[(tile + 1) * MOE_T + r]
        return pltpu.make_async_copy(obuf.at[slot, pl.ds(r, 1)], y_ref.at[pl.ds(dst, 1)],
                                     ssem.at[slot])

    def scatter_all(slot):
        return pltpu.make_async_copy(obuf.at[slot], y_ref.at[pl.ds(0, MOE_T)], ssem.at[slot])

    @pl.when(jnp.logical_and(j == 0, f == 0))
    def _():
        obuf[...] = jnp.zeros_like(obuf)
        for s in range(N_EXP):
            spare = pltpu.make_async_copy(
                obuf.at[0], y_ref.at[pl.ds(2 * NTOK + s * MOE_T, MOE_T)], ssem.at[0])
            spare.start()
            spare.wait()

        def first(r, c):
            gather(0, 0, r).start()
            scatter(-1, 0, r).start()
            return c

        lax.fori_loop(0, MOE_T, first, 0)

    @pl.when(jnp.logical_and(computes, f == 0))
    def _():
        gather_all(cur).wait()
        xb[...] = xbuf[cur].astype(BF16)
        for r in range(MOE_T):
            gather(j + 1, nxt, r).start()
        x = xb[...]
        u = jnp.dot(x, wu_ref[...], preferred_element_type=F32)
        g = jnp.dot(x, wg_ref[...], preferred_element_type=F32)
        a = (_silu(u) * g).astype(BF16)
        y = jnp.dot(a, w2_ref[...], preferred_element_type=F32)
        scatter_all(cur).wait()
        obuf[cur] = y

    @pl.when(jnp.logical_and(computes, f == 1))
    def _():
        for r in range(MOE_T):
            scatter(j - 1, nxt, r).start()
        x = xb[...]
        u = jnp.dot(x, wu_ref[...], preferred_element_type=F32)
        g = jnp.dot(x, wg_ref[...], preferred_element_type=F32)
        a = (_silu(u) * g).astype(BF16)
        y = jnp.dot(a, w2_ref[...], preferred_element_type=F32)
        obuf[cur] = obuf[cur] + y

    @pl.when(jnp.logical_and(drains, f == 0))
    def _():
        gather_all(cur).wait()
        scatter_all(cur).wait()

    @pl.when(jnp.logical_and(drains, f == 1))
    def _():
        def last(r, c):
            scatter(j - 1, nxt, r).start()
            return c

        lax.fori_loop(0, MOE_T, last, 0)
        scatter_all(nxt).wait()


def _experts(tile_e, tile_v, rowmap, h, w13, w2):
    def fidx(j, f, tv):
        return jnp.where(tv[j + 1] == 1, f, MOE_NF - 1)

    return pl.pallas_call(
        _expert_kernel,
        out_shape=jax.ShapeDtypeStruct((Y_ROWS, D), F32),
        grid_spec=pltpu.PrefetchScalarGridSpec(
            num_scalar_prefetch=3,
            grid=(MOE_STEPS, MOE_NF),
            in_specs=[
                pl.BlockSpec(memory_space=pl.ANY),
                pl.BlockSpec((None, D, MOE_FT), lambda j, f, te, tv, *_: (te[j], 0, fidx(j, f, tv))),
                pl.BlockSpec((None, D, MOE_FT),
                             lambda j, f, te, tv, *_: (te[j], 0, MOE_NF + fidx(j, f, tv))),
                pl.BlockSpec((None, MOE_FT, D), lambda j, f, te, tv, *_: (te[j], fidx(j, f, tv), 0)),
            ],
            out_specs=pl.BlockSpec(memory_space=pl.ANY),
            scratch_shapes=[pltpu.VMEM((2, MOE_T, D), F32), pltpu.VMEM((MOE_T, D), BF16),
                            pltpu.VMEM((2, MOE_T, D), F32),
                            pltpu.SemaphoreType.DMA((2,)), pltpu.SemaphoreType.DMA((2,))]),
        compiler_params=_cparams(2),
        name="moe_experts",
    )(tile_e, tile_v, rowmap, h, w13, w13, w2)


def _rowmap_kernel(lo_ref, hi_ref, pos_ref, map_ref):
    def fill(start, stop, value_at):
        def body(i, c):
            map_ref[i] = value_at(i)
            return c
        lax.fori_loop(start, stop, body, 0)

    fill(0, MOE_T, lambda i: Y_IDLE + i)
    for e in range(N_EXP):
        lo = lo_ref[e]
        fill(MOE_T + lo, MOE_T + hi_ref[e], lambda i: 2 * NTOK + e * MOE_T + (i - MOE_T - lo))
    fill(MOE_T + hi_ref[N_EXP - 1], MAP_ROWS, lambda i: Y_IDLE + (i & (MOE_T - 1)))

    def token(t, c):
        map_ref[MOE_T + pos_ref[t]] = t
        map_ref[MOE_T + pos_ref[NTOK + t]] = NTOK + t
        return c

    lax.fori_loop(0, NTOK, token, 0, unroll=8)


def _rowmap(pad_lo, pad_hi, pos):
    smem = pl.BlockSpec(memory_space=pltpu.SMEM)
    return pl.pallas_call(
        _rowmap_kernel,
        out_shape=jax.ShapeDtypeStruct((MAP_ROWS,), jnp.int32),
        in_specs=[smem, smem, smem],
        out_specs=smem,
        name="moe_rowmap",
    )(pad_lo, pad_hi, pos)


def _combine_kernel(y1_ref, y2_ref, meta_ref, x_ref, mod_ref, fg_ref, yp_ref, ys_ref):
    i = pl.program_id(0)
    meta = meta_ref[...]
    y = meta[:, 4:5] * y1_ref[...] + meta[:, 5:6] * y2_ref[...]
    x = x_ref[...] + mod_ref[5:6] * y
    out = x * lax.rsqrt(jnp.mean(x * x, axis=-1, keepdims=True) + EPS) * fg_ref[...]

    @pl.when(i < ROW_PT)
    def _():
        yp_ref[...] = out

    @pl.when(i >= ROW_PT)
    def _():
        ys_ref[...] = out


def _combine(y, meta, x, mods, final_g):
    slot2 = NTOK // ROW_T
    return pl.pallas_call(
        _combine_kernel,
        out_shape=(jax.ShapeDtypeStruct((NP_TOK, D), F32),
                   jax.ShapeDtypeStruct((NS_TOK, D), F32)),
        grid=(NTOK // ROW_T,),
        in_specs=[pl.BlockSpec((ROW_T, D), lambda i: (i, 0)),
                  pl.BlockSpec((ROW_T, D), lambda i: (slot2 + i, 0)),
                  pl.BlockSpec((ROW_T, LANES), lambda i: (i, 0)),
                  pl.BlockSpec((ROW_T, D), lambda i: (i, 0)),
                  _mod_spec(1, ROW_T),
                  _const_spec((1, D))],
        out_specs=(pl.BlockSpec((ROW_T, D), lambda i: (jnp.minimum(i, ROW_PT - 1), 0)),
                   pl.BlockSpec((ROW_T, D), lambda i: (jnp.maximum(i - ROW_PT, 0), 0))),
        compiler_params=_cparams(1),
        name="moe_combine",
    )(y, y, meta, x, mods, final_g)


def _routing_tables(meta, counts):
    cnt = counts[0, :N_EXP].astype(jnp.int32)
    padded = (cnt + MOE_T - 1) // MOE_T * MOE_T
    ends = jnp.cumsum(padded)
    off = ends - padded
    experts = jnp.arange(N_EXP, dtype=jnp.int32)

    def group_start(e):
        return jnp.sum(jnp.where(e[:, None] == experts[None, :], off[None, :], 0), axis=1)

    pos = jnp.concatenate([group_start(meta[:, 0].astype(jnp.int32)) + meta[:, 2].astype(jnp.int32),
                           group_start(meta[:, 1].astype(jnp.int32)) + meta[:, 3].astype(jnp.int32)])
    n_valid = ends[-1] // MOE_T
    tile = jnp.arange(MOE_STEPS, dtype=jnp.int32)
    first_row = jnp.minimum(tile, n_valid - 1) * MOE_T
    tile_e = jnp.sum((ends[None, :] <= first_row[:, None]).astype(jnp.int32), axis=1)
    tile_e = jnp.minimum(tile_e, N_EXP - 1)
    tile_v = jnp.concatenate([jnp.zeros((1,), jnp.int32), (tile < n_valid).astype(jnp.int32)])
    return tile_e, tile_v, off + cnt, ends, pos


def kernel(x_prompt, x_sample, cache_k_l1, cache_v_l1, c, c_ctx, ada_w, ada_b, norm_g, conv_w1, conv_b1, conv_dw, conv_dwb, conv_ln_g, conv_ln_b, conv_w2, attn_wqkv, attn_wo, attn_sink, ffn_w13, ffn_w2, moe_router, moe_w13, moe_w2, final_g):
    xp = x_prompt.reshape(NP_TOK, D)
    xs = x_sample.reshape(NS_TOK, D)
    cond = jnp.concatenate([c, c_ctx[None, :], jnp.zeros((N_COND - DEC_BATCH - 1, D), F32)])
    mods = _ada(cond, ada_w, ada_b)

    row = lambda a: a.reshape(1, -1)
    x1, (ffn_w13_b, ffn_w2_b, wqkv_b, wo_b, moe_w13_b) = _conv_layer(
        xp, xs, mods, row(norm_g[0, 0]), conv_w1.astype(BF16), row(conv_b1),
        conv_dw, row(conv_dwb), row(conv_ln_g), row(conv_ln_b), conv_w2.astype(BF16),
        [ffn_w13, ffn_w2, attn_wqkv, attn_wo, moe_w13.reshape(N_EXP * D, 2 * D_FFE)])
    x2, moe_w2_b = _ffn_layer(x1, mods, row(norm_g[0, 1]), ffn_w13_b, ffn_w2_b,
                              moe_w2.reshape(N_EXP * D_FFE, D))
    moe_w13_b = moe_w13_b.reshape(N_EXP, D, 2 * D_FFE)
    moe_w2_b = moe_w2_b.reshape(N_EXP, D_FFE, D)

    cos, sin = _rope_tables()
    q, k, v, state_k, state_v = _qkv_layer(x2, mods, row(norm_g[1, 0]), wqkv_b, cos, sin)
    o_ctx, o_lat = _attention(q, k, v, cache_k_l1.reshape(DEC_BATCH, PAST, KV_COLS),
                              cache_v_l1.reshape(DEC_BATCH, PAST, KV_COLS), attn_sink)

    x3, h, meta, counts = _wo_router(o_ctx, o_lat, x2, mods, row(norm_g[1, 1]), wo_b,
                                     moe_router.T)
    tile_e, tile_v, pad_lo, pad_hi, pos = _routing_tables(meta, counts)
    rowmap = _rowmap(pad_lo, pad_hi, pos)
    y = _experts(tile_e, tile_v, rowmap, h, moe_w13_b, moe_w2_b)
    y_p, y_s = _combine(y, meta, x3, mods, row(final_g))

    return (y_p.reshape(BATCH, SEQ, D), y_s.reshape(DEC_BATCH, DEC_SEQ, D),
            state_k.reshape(BATCH, SEQ, N_KV, HD), state_v.reshape(BATCH, SEQ, N_KV, HD))
```

```python
import functools

import jax
import jax.numpy as jnp
from jax import lax
from jax.experimental import pallas as pl
from jax.experimental.pallas import tpu as pltpu

F32 = jnp.float32
BF16 = jnp.bfloat16

D = 1024
BATCH, SEQ = 32, 256
DEC_BATCH, DEC_SEQ = 4, 2048
PAST = 256
NP_TOK = BATCH * SEQ
NS_TOK = DEC_BATCH * DEC_SEQ
NTOK = NP_TOK + NS_TOK
GRID_W = 64
N_HEADS, N_KV, HD = 16, 4, 64
GQA = N_HEADS // N_KV
KV_COLS = N_KV * HD
QKV_COLS = D + 2 * KV_COLS
WINDOW = 128
ROPE_THETA = 10000.0
CONV_W = 31
CONV_PAD = CONV_W // 2
D_FF = 2816
N_EXP = 8
D_FFE = 3584
N_MOD = 6
EPS = 1e-6
NEG_INF = -1e30
LOG2E = 1.4426950408889634

N_COND = 8
CTX_ROW = DEC_BATCH

CONV_T = 256
HALO = 16
ROW_T = 512
ATT_T = 256
ATT_WIN = ATT_T + 2 * WINDOW
FF_T = 1408
MOE_T = 512
MOE_FT = 1792
MOE_TILES = 2 * NTOK // MOE_T + N_EXP
MOE_ROWS = MOE_TILES * MOE_T
LANES = 128

VMEM_LIMIT = 56 * 1024 * 1024


def _cparams(n_axes=1, vmem=VMEM_LIMIT):
    return pltpu.CompilerParams(dimension_semantics=("arbitrary",) * n_axes,
                                vmem_limit_bytes=vmem)


def _silu(x):
    return x * jax.nn.sigmoid(x)


def _norm_mod(x, g, shift, scale):
    y = x * lax.rsqrt(jnp.mean(x * x, axis=-1, keepdims=True) + EPS)
    return (y * g) * (1.0 + scale) + shift


def _cond_row(tile, tile_rows):
    n_prompt_tiles = NP_TOK // tile_rows
    per_seq = DEC_SEQ // tile_rows
    return jnp.where(tile < n_prompt_tiles, CTX_ROW, (tile - n_prompt_tiles) // per_seq)


def _mod_spec(layer, tile_rows):
    return pl.BlockSpec((None, None, N_MOD, D),
                        lambda i, *_: (layer, _cond_row(i, tile_rows), 0, 0))


def _const_spec(shape):
    nd = len(shape)
    return pl.BlockSpec(shape, lambda *_: (0,) * nd)


BF16_SUBLANES = 16


def _cast_specs(arrays, steps):
    specs, shapes = [], []
    for a in arrays:
        rows, cols = a.shape
        n = steps
        while rows % n or (rows // n) % BF16_SUBLANES:
            n -= 1
        specs.append(pl.BlockSpec((rows // n, cols), lambda i, n=n: (jnp.minimum(i, n - 1), 0)))
        shapes.append(jax.ShapeDtypeStruct((rows, cols), BF16))
    return specs, shapes


def _cast_blocks(in_refs, out_refs):
    for w_ref, o_ref in zip(in_refs, out_refs):
        o_ref[...] = w_ref[...].astype(BF16)


ADA_TN = 1536


def _ada_kernel(cond_ref, w_ref, b_ref, o_ref):
    s = _silu(cond_ref[...]).astype(BF16)
    o_ref[...] = jnp.dot(s, w_ref[...].astype(BF16), preferred_element_type=F32) + b_ref[...]


def _ada(cond, ada_w, ada_b):
    depth = ada_w.shape[0]
    out = pl.pallas_call(
        _ada_kernel,
        out_shape=jax.ShapeDtypeStruct((depth, N_COND, N_MOD * D), F32),
        grid=(depth, N_MOD * D // ADA_TN),
        in_specs=[pl.BlockSpec((N_COND, D), lambda l, n: (0, 0)),
                  pl.BlockSpec((None, D, ADA_TN), lambda l, n: (l, 0, n)),
                  pl.BlockSpec((None, 1, ADA_TN), lambda l, n: (l, 0, n))],
        out_specs=pl.BlockSpec((None, N_COND, ADA_TN), lambda l, n: (l, 0, n)),
        compiler_params=_cparams(2),
        name="adaln",
    )(cond, ada_w, ada_b.reshape(depth, 1, N_MOD * D))
    return out.reshape(depth, N_COND, N_MOD, D)


CONV_PT = NP_TOK // CONV_T
CONV_TPS = DEC_SEQ // CONV_T
CONV_E = CONV_T + 2 * HALO


CONV_NIN = 13


def _conv_kernel(*refs, n_cast):
    (xp_ref, xs_ref, xprev_ref, xnext_ref, mod_ref, ng_ref, w1_ref, b1_ref,
     dw_ref, dwb_ref, lng_ref, lnb_ref, w2_ref) = refs[:CONV_NIN]
    o_ref = refs[CONV_NIN + n_cast]
    g_scr, y_scr = refs[-2:]
    _cast_blocks(refs[CONV_NIN:CONV_NIN + n_cast], refs[CONV_NIN + n_cast + 1:-2])
    i = pl.program_id(0)
    j = i - CONV_PT
    is_prompt = i < CONV_PT
    has_prev = jnp.logical_and(i >= CONV_PT, j % CONV_TPS != 0)
    has_next = jnp.logical_and(i >= CONV_PT, j % CONV_TPS != CONV_TPS - 1)
    x_main = jnp.where(is_prompt, xp_ref[...], xs_ref[...])
    shift, scale, gate = mod_ref[0:1], mod_ref[1:2], mod_ref[2:3]

    xe = jnp.concatenate([xprev_ref[...], x_main, xnext_ref[...]], axis=0)
    h = _norm_mod(xe, ng_ref[...], shift, scale).astype(BF16)
    uv = jnp.dot(h, w1_ref[...], preferred_element_type=F32) + b1_ref[...]
    g = uv[:, :D] * jax.nn.sigmoid(uv[:, D:])
    g_scr[0:HALO] = jnp.where(has_prev, g[0:HALO], 0.0)
    g_scr[HALO:HALO + CONV_T] = g[HALO:HALO + CONV_T]
    g_scr[HALO + CONV_T:] = jnp.where(has_next, g[HALO + CONV_T:], 0.0)

    base = HALO - CONV_PAD
    sub = 8
    for c in range(D // LANES):
        cs = slice(c * LANES, (c + 1) * LANES)
        y = dwb_ref[:, cs]
        for r in range(sub):
            z = None
            for a in range((base + CONV_W - 1) // sub + 1):
                k = sub * a + r - base
                if 0 <= k < CONV_W:
                    term = g_scr[sub * a:sub * a + CONV_T + sub, cs] * dw_ref[k:k + 1, cs]
                    z = term if z is None else z + term
            y = y + z[r:r + CONV_T]
        y_scr[:, cs] = y

    y = y_scr[...]
    mu = jnp.mean(y, axis=-1, keepdims=True)
    d = y - mu
    var = jnp.mean(d * d, axis=-1, keepdims=True)
    yn = d * lax.rsqrt(var + EPS) * lng_ref[...] + lnb_ref[...]
    o = jnp.dot(_silu(yn).astype(BF16), w2_ref[...], preferred_element_type=F32)
    o_ref[...] = x_main + gate * o


def _conv_layer(xp, xs, mods, ng, w1, b1, dw, dwb, lng, lnb, w2, to_cast):
    n_tiles = NTOK // CONV_T
    hb = CONV_T // HALO
    n_hb = NS_TOK // HALO
    cast_specs, cast_shapes = _cast_specs(to_cast, n_tiles)

    def s_tile(i):
        return jnp.maximum(i - CONV_PT, 0)

    outs = pl.pallas_call(
        functools.partial(_conv_kernel, n_cast=len(to_cast)),
        out_shape=[jax.ShapeDtypeStruct((NTOK, D), F32)] + cast_shapes,
        grid=(n_tiles,),
        in_specs=[
            pl.BlockSpec((CONV_T, D), lambda i: (jnp.minimum(i, CONV_PT - 1), 0)),
            pl.BlockSpec((CONV_T, D), lambda i: (s_tile(i), 0)),
            pl.BlockSpec((HALO, D), lambda i: (jnp.maximum(s_tile(i) * hb - 1, 0), 0)),
            pl.BlockSpec((HALO, D), lambda i: (jnp.minimum(s_tile(i) * hb + hb, n_hb - 1), 0)),
            _mod_spec(0, CONV_T),
            _const_spec((1, D)),
            _const_spec((D, 2 * D)),
            _const_spec((1, 2 * D)),
            _const_spec((CONV_W, D)),
            _const_spec((1, D)),
            _const_spec((1, D)),
            _const_spec((1, D)),
            _const_spec((D, D)),
        ] + cast_specs,
        out_specs=[pl.BlockSpec((CONV_T, D), lambda i: (i, 0))] + cast_specs,
        scratch_shapes=[pltpu.VMEM((CONV_E, D), F32), pltpu.VMEM((CONV_T, D), F32)],
        compiler_params=_cparams(1),
        name="conv_mixer",
    )(xp, xs, xs, xs, mods, ng, w1, b1, dw, dwb, lng, lnb, w2, *to_cast)
    return outs[0], outs[1:]


def _ffn_kernel(x_ref, mod_ref, ng_ref, w13_ref, w2_ref, cast_ref, o_ref, cast_out_ref):
    _cast_blocks([cast_ref], [cast_out_ref])
    x = x_ref[...]
    h = _norm_mod(x, ng_ref[...], mod_ref[3:4], mod_ref[4:5]).astype(BF16)
    acc = jnp.zeros((ROW_T, D), F32)
    for f in range(D_FF // FF_T):
        u = jnp.dot(h, w13_ref[:, f * FF_T:(f + 1) * FF_T], preferred_element_type=F32)
        g = jnp.dot(h, w13_ref[:, D_FF + f * FF_T:D_FF + (f + 1) * FF_T],
                    preferred_element_type=F32)
        a = (_silu(u) * g).astype(BF16)
        acc = acc + jnp.dot(a, w2_ref[f * FF_T:(f + 1) * FF_T, :], preferred_element_type=F32)
    o_ref[...] = x + mod_ref[5:6] * acc


def _ffn_layer(x, mods, ng, w13, w2, to_cast):
    steps = NTOK // ROW_T
    cast_specs, cast_shapes = _cast_specs([to_cast], steps)
    return pl.pallas_call(
        _ffn_kernel,
        out_shape=[jax.ShapeDtypeStruct((NTOK, D), F32)] + cast_shapes,
        grid=(steps,),
        in_specs=[pl.BlockSpec((ROW_T, D), lambda i: (i, 0)),
                  _mod_spec(0, ROW_T),
                  _const_spec((1, D)),
                  pl.BlockSpec((D, 2 * D_FF), lambda i: (0, 0), pipeline_mode=pl.Buffered(1)),
                  pl.BlockSpec((D_FF, D), lambda i: (0, 0), pipeline_mode=pl.Buffered(1))]
        + cast_specs,
        out_specs=[pl.BlockSpec((ROW_T, D), lambda i: (i, 0))] + cast_specs,
        compiler_params=_cparams(1),
        name="dense_ffn",
    )(x, mods, ng, w13, w2, to_cast)


ROW_PT = NP_TOK // ROW_T
ROW_TPS = DEC_SEQ // ROW_T


def _rope_tables():
    half = HD // 2
    inv = ROPE_THETA ** (-jnp.arange(0, half, 2, dtype=F32) / half)
    t = jnp.arange(DEC_SEQ)
    lane = jnp.arange(LANES)
    jj = lane % HD
    pos = jnp.where(jj[None, :] < half, (t // GRID_W)[:, None], (t % GRID_W)[:, None]).astype(F32)
    ang = pos * inv[jj % (half // 2)][None, :]
    first = (jj % half) < (half // 2)
    cos = jnp.cos(ang)
    sin = jnp.where(first[None, :], -jnp.sin(ang), jnp.sin(ang))
    cos = jnp.concatenate([cos.reshape(ROW_TPS, ROW_T, LANES), jnp.ones((1, ROW_T, LANES), F32)])
    sin = jnp.concatenate([sin.reshape(ROW_TPS, ROW_T, LANES), jnp.zeros((1, ROW_T, LANES), F32)])
    return cos, sin


def _rope(x, cos, sin):
    lane = lax.broadcasted_iota(jnp.int32, x.shape, 1)
    first = (lane % (HD // 2)) < (HD // 4)
    swapped = jnp.where(first, pltpu.roll(x, LANES - HD // 4, axis=1), pltpu.roll(x, HD // 4, axis=1))
    return x * cos + swapped * sin


def _qkv_kernel(x_ref, mod_ref, ng_ref, w_ref, cos_ref, sin_ref,
                q_ref, k_ref, v_ref, sk_ref, sv_ref):
    i = pl.program_id(0)
    h = _norm_mod(x_ref[...], ng_ref[...], mod_ref[0:1], mod_ref[1:2]).astype(BF16)
    y = jnp.dot(h, w_ref[...], preferred_element_type=F32)
    k = y[:, D:D + KV_COLS]
    v = y[:, D + KV_COLS:]

    @pl.when(i < ROW_PT)
    def _():
        sk_ref[...] = k
        sv_ref[...] = v

    cos, sin = cos_ref[...], sin_ref[...]
    scale = HD ** -0.5 * LOG2E
    for c in range(D // LANES):
        cs = slice(c * LANES, (c + 1) * LANES)
        q_ref[:, cs] = (_rope(y[:, cs], cos, sin) * scale).astype(BF16)
    for c in range(KV_COLS // LANES):
        cs = slice(c * LANES, (c + 1) * LANES)
        k_ref[:, cs] = _rope(k[:, cs], cos, sin).astype(BF16)
    v_ref[...] = v.astype(BF16)


def _qkv_layer(x, mods, ng, wqkv, cos, sin):
    def tab(i):
        return (jnp.where(i < ROW_PT, ROW_TPS, (i - ROW_PT) % ROW_TPS), 0, 0)

    def state(i):
        return (jnp.minimum(i, ROW_PT - 1), 0)

    return pl.pallas_call(
        _qkv_kernel,
        out_shape=(jax.ShapeDtypeStruct((NTOK, D), BF16),
                   jax.ShapeDtypeStruct((NTOK, KV_COLS), BF16),
                   jax.ShapeDtypeStruct((NTOK, KV_COLS), BF16),
                   jax.ShapeDtypeStruct((NP_TOK, KV_COLS), F32),
                   jax.ShapeDtypeStruct((NP_TOK, KV_COLS), F32)),
        grid=(NTOK // ROW_T,),
        in_specs=[pl.BlockSpec((ROW_T, D), lambda i: (i, 0)),
                  _mod_spec(1, ROW_T),
                  _const_spec((1, D)),
                  _const_spec((D, QKV_COLS)),
                  pl.BlockSpec((None, ROW_T, LANES), tab),
                  pl.BlockSpec((None, ROW_T, LANES), tab)],
        out_specs=(pl.BlockSpec((ROW_T, D), lambda i: (i, 0)),
                   pl.BlockSpec((ROW_T, KV_COLS), lambda i: (i, 0)),
                   pl.BlockSpec((ROW_T, KV_COLS), lambda i: (i, 0)),
                   pl.BlockSpec((ROW_T, KV_COLS), state),
                   pl.BlockSpec((ROW_T, KV_COLS), state)),
        compiler_params=_cparams(1),
        name="qkv_rope",
    )(x, mods, ng, wqkv, cos, sin)


def _attn_heads(q, parts, sink_ref, o_ref):
    nt = (((1,), (1,)), ((), ()))
    tq = q.shape[0]
    for h in range(N_HEADS):
        kv = h // GQA
        hs = slice(kv * HD, (kv + 1) * HD)
        qh = q[:, h * HD:(h + 1) * HD]
        sink = sink_ref[h] * LOG2E
        scores = []
        m = jnp.full((tq, 1), sink, F32)
        for k, _, bias in parts:
            s = lax.dot_general(qh, k[:, hs], nt, preferred_element_type=F32)
            if bias is not None:
                s = s + bias
            scores.append(s)
            m = jnp.maximum(m, jnp.max(s, axis=-1, keepdims=True))
        denom = jnp.exp2(sink - m)
        o = jnp.zeros((tq, HD), F32)
        for s, (_, v, _) in zip(scores, parts):
            p = jnp.exp2(s - m)
            denom = denom + jnp.sum(p, axis=-1, keepdims=True)
            o = o + jnp.dot(p.astype(BF16), v[:, hs], preferred_element_type=F32)
        o_ref[:, h * HD:(h + 1) * HD] = (o / denom).astype(o_ref.dtype)


def _ctx_attn_kernel(sink_ref, q_ref, k_ref, v_ref, o_ref):
    _attn_heads(q_ref[...], [(k_ref[...], v_ref[...], None)], sink_ref, o_ref)


def _lat_attn_kernel(sink_ref, q_ref, k_ref, v_ref, ck_ref, cv_ref, o_ref):
    j = pl.program_id(1)
    w0 = pl.multiple_of(jnp.clip(j * ATT_T - WINDOW, 0, DEC_SEQ - ATT_WIN), WINDOW)
    qpos = j * ATT_T + lax.broadcasted_iota(jnp.int32, (ATT_T, ATT_WIN), 0)
    kpos = w0 + lax.broadcasted_iota(jnp.int32, (ATT_T, ATT_WIN), 1)
    bias = jnp.where(jnp.abs(qpos - kpos) <= WINDOW, 0.0, NEG_INF)
    parts = [(k_ref[pl.ds(w0, ATT_WIN), :], v_ref[pl.ds(w0, ATT_WIN), :], bias),
             (ck_ref[...].astype(BF16), cv_ref[...].astype(BF16), None)]
    _attn_heads(q_ref[...], parts, sink_ref, o_ref)


def _attention(q, k, v, ck, cv, sink):
    smem = pl.BlockSpec(memory_space=pltpu.SMEM)
    o_ctx = pl.pallas_call(
        _ctx_attn_kernel,
        out_shape=jax.ShapeDtypeStruct((NP_TOK, D), BF16),
        grid=(BATCH,),
        in_specs=[smem,
                  pl.BlockSpec((SEQ, D), lambda b: (b, 0)),
                  pl.BlockSpec((SEQ, KV_COLS), lambda b: (b, 0)),
                  pl.BlockSpec((SEQ, KV_COLS), lambda b: (b, 0))],
        out_specs=pl.BlockSpec((SEQ, D), lambda b: (b, 0)),
        compiler_params=_cparams(1),
        name="ctx_attention",
    )(sink, q, k, v)

    q_off = NP_TOK // ATT_T
    kv_off = NP_TOK // DEC_SEQ
    tps = DEC_SEQ // ATT_T
    o_lat = pl.pallas_call(
        _lat_attn_kernel,
        out_shape=jax.ShapeDtypeStruct((NS_TOK, D), BF16),
        grid=(DEC_BATCH, tps),
        in_specs=[smem,
                  pl.BlockSpec((ATT_T, D), lambda b, j: (q_off + b * tps + j, 0)),
                  pl.BlockSpec((DEC_SEQ, KV_COLS), lambda b, j: (kv_off + b, 0)),
                  pl.BlockSpec((DEC_SEQ, KV_COLS), lambda b, j: (kv_off + b, 0)),
                  pl.BlockSpec((None, PAST, KV_COLS), lambda b, j: (b, 0, 0)),
                  pl.BlockSpec((None, PAST, KV_COLS), lambda b, j: (b, 0, 0))],
        out_specs=pl.BlockSpec((ATT_T, D), lambda b, j: (b * tps + j, 0)),
        compiler_params=_cparams(2),
        name="latent_attention",
    )(sink, q, k, v, ck, cv)
    return o_ctx, o_lat


def _wo_router_kernel(oc_ref, ol_ref, x_ref, mod_ref, ng_ref, wo_ref, r_ref,
                      x3_ref, h_ref, meta_ref, cnt_ref, carry):
    i = pl.program_id(0)

    @pl.when(i == 0)
    def _():
        carry[...] = jnp.zeros_like(carry)

    a = jnp.where(i < ROW_PT, oc_ref[...], ol_ref[...])
    o = jnp.dot(a, wo_ref[...], preferred_element_type=F32)
    x3 = x_ref[...] + mod_ref[2:3] * o
    x3_ref[...] = x3
    h = _norm_mod(x3, ng_ref[...], mod_ref[3:4], mod_ref[4:5])
    h_ref[...] = h

    logit = [jnp.sum(h * r_ref[e:e + 1, :], axis=-1, keepdims=True) for e in range(N_EXP)]

    def top1(cols):
        m = functools.reduce(jnp.maximum, cols)
        idx = jnp.full_like(m, float(N_EXP))
        for e in reversed(range(N_EXP)):
            idx = jnp.where(cols[e] == m, float(e), idx)
        return m, idx

    m1, e1 = top1(logit)
    m2, e2 = top1([jnp.where(e1 == float(e), -jnp.inf, logit[e]) for e in range(N_EXP)])
    lane = lax.broadcasted_iota(jnp.int32, (ROW_T, LANES), 1).astype(F32)
    t = jnp.exp(m2 - m1)
    g1 = 1.0 / (1.0 + t)
    g2 = t / (1.0 + t)

    oh1 = (lane == e1).astype(F32)
    oh2 = (lane == e2).astype(F32)
    oh = oh1 + oh2
    r_i = lax.broadcasted_iota(jnp.int32, (ROW_T, ROW_T), 0)
    c_i = lax.broadcasted_iota(jnp.int32, (ROW_T, ROW_T), 1)
    tri = (c_i < r_i).astype(BF16)
    before = jnp.dot(tri, oh.astype(BF16), preferred_element_type=F32) + carry[0:1, :]
    rank1 = jnp.sum(before * oh1, axis=-1, keepdims=True)
    rank2 = jnp.sum(before * oh2, axis=-1, keepdims=True)
    carry[0:1, :] = carry[0:1, :] + jnp.sum(oh, axis=0, keepdims=True)
    cnt_ref[...] = carry[...]

    meta = jnp.where(lane == 0, e1, 0.0)
    meta = jnp.where(lane == 1, e2, meta)
    meta = jnp.where(lane == 2, rank1, meta)
    meta = jnp.where(lane == 3, rank2, meta)
    meta = jnp.where(lane == 4, g1, meta)
    meta = jnp.where(lane == 5, g2, meta)
    meta_ref[...] = meta


def _wo_router(o_ctx, o_lat, x, mods, ng, wo, router_t):
    return pl.pallas_call(
        _wo_router_kernel,
        out_shape=(jax.ShapeDtypeStruct((NTOK, D), F32),
                   jax.ShapeDtypeStruct((NTOK, D), F32),
                   jax.ShapeDtypeStruct((NTOK, LANES), F32),
                   jax.ShapeDtypeStruct((8, LANES), F32)),
        grid=(NTOK // ROW_T,),
        in_specs=[pl.BlockSpec((ROW_T, D), lambda i: (jnp.minimum(i, ROW_PT - 1), 0)),
                  pl.BlockSpec((ROW_T, D), lambda i: (jnp.maximum(i - ROW_PT, 0), 0)),
                  pl.BlockSpec((ROW_T, D), lambda i: (i, 0)),
                  _mod_spec(1, ROW_T),
                  _const_spec((1, D)),
                  _const_spec((D, D)),
                  _const_spec((N_EXP, D))],
        out_specs=(pl.BlockSpec((ROW_T, D), lambda i: (i, 0)),
                   pl.BlockSpec((ROW_T, D), lambda i: (i, 0)),
                   pl.BlockSpec((ROW_T, LANES), lambda i: (i, 0)),
                   pl.BlockSpec((8, LANES), lambda i: (0, 0))),
        scratch_shapes=[pltpu.VMEM((8, LANES), F32)],
        compiler_params=_cparams(1),
        name="wo_router",
    )(o_ctx, o_lat, x, mods, ng, wo, router_t)


MOE_NF = D_FFE // MOE_FT
assert MOE_NF == 2
MOE_STEPS = MOE_TILES + 1
Y_IDLE = 2 * NTOK + N_EXP * MOE_T
Y_ROWS = Y_IDLE + MOE_T
MAP_ROWS = (MOE_TILES + 2) * MOE_T
assert NTOK & (NTOK - 1) == 0


def _expert_kernel(te_ref, tv_ref, map_ref, h_ref, wu_ref, wg_ref, w2_ref, y_ref,
                   xbuf, xb, obuf, gsem, ssem):
    j = pl.program_id(0)
    f = pl.program_id(1)
    cur = j % 2
    nxt = 1 - cur
    computes = tv_ref[j + 1] == 1
    drains = jnp.logical_and(tv_ref[j] == 1, jnp.logical_not(computes))

    def gather(tile, slot, r):
        src = map_ref[(tile + 1) * MOE_T + r] & (NTOK - 1)
        return pltpu.make_async_copy(h_ref.at[pl.ds(src, 1)], xbuf.at[slot, pl.ds(r, 1)],
                                     gsem.at[slot])

    def gather_all(slot):
        return pltpu.make_async_copy(h_ref.at[pl.ds(0, MOE_T)], xbuf.at[slot], gsem.at[slot])

    def scatter(tile, slot, r):
        dst = map_ref[(tile + 1) * MOE_T + r]
        return pltpu.make_async_copy(obuf.at[slot, pl.ds(r, 1)], y_ref.at[pl.ds(dst, 1)],
                                     ssem.at[slot])

    def scatter_all(slot):
        return pltpu.make_async_copy(obuf.at[slot], y_ref.at[pl.ds(0, MOE_T)], ssem.at[slot])

    @pl.when(jnp.logical_and(j == 0, f == 0))
    def _():
        obuf[...] = jnp.zeros_like(obuf)
        for s in range(N_EXP):
            spare = pltpu.make_async_copy(
                obuf.at[0], y_ref.at[pl.ds(2 * NTOK + s * MOE_T, MOE_T)], ssem.at[0])
            spare.start()
            spare.wait()

        def first(r, c):
            gather(0, 0, r).start()
            scatter(-1, 0, r).start()
            return c

        lax.fori_loop(0, MOE_T, first, 0)

    @pl.when(jnp.logical_and(computes, f == 0))
    def _():
        gather_all(cur).wait()
        xb[...] = xbuf[cur].astype(BF16)
        for r in range(MOE_T):
            gather(j + 1, nxt, r).start()
        x = xb[...]
        u = jnp.dot(x, wu_ref[...], preferred_element_type=F32)
        g = jnp.dot(x, wg_ref[...], preferred_element_type=F32)
        a = (_silu(u) * g).astype(BF16)
        y = jnp.dot(a, w2_ref[...], preferred_element_type=F32)
        scatter_all(cur).wait()
        obuf[cur] = y

    @pl.when(jnp.logical_and(computes, f == 1))
    def _():
        for r in range(MOE_T):
            scatter(j - 1, nxt, r).start()
        x = xb[...]
        u = jnp.dot(x, wu_ref[...], preferred_element_type=F32)
        g = jnp.dot(x, wg_ref[...], preferred_element_type=F32)
        a = (_silu(u) * g).astype(BF16)
        y = jnp.dot(a, w2_ref[...], preferred_element_type=F32)
        obuf[cur] = obuf[cur] + y

    @pl.when(jnp.logical_and(drains, f == 0))
    def _():
        gather_all(cur).wait()
        scatter_all(cur).wait()

    @pl.when(jnp.logical_and(drains, f == 1))
    def _():
        def last(r, c):
            scatter(j - 1, nxt, r).start()
            return c

        lax.fori_loop(0, MOE_T, last, 0)
        scatter_all(nxt).wait()


def _experts(tile_e, tile_v, rowmap, h, w13, w2):
    def fidx(j, f, tv):
        return jnp.where(tv[j + 1] == 1, f, MOE_NF - 1)

    return pl.pallas_call(
        _expert_kernel,
        out_shape=jax.ShapeDtypeStruct((Y_ROWS, D), F32),
        grid_spec=pltpu.PrefetchScalarGridSpec(
            num_scalar_prefetch=3,
            grid=(MOE_STEPS, MOE_NF),
            in_specs=[
                pl.BlockSpec(memory_space=pl.ANY),
                pl.BlockSpec((None, D, MOE_FT), lambda j, f, te, tv, *_: (te[j], 0, fidx(j, f, tv))),
                pl.BlockSpec((None, D, MOE_FT),
                             lambda j, f, te, tv, *_: (te[j], 0, MOE_NF + fidx(j, f, tv))),
                pl.BlockSpec((None, MOE_FT, D), lambda j, f, te, tv, *_: (te[j], fidx(j, f, tv), 0)),
            ],
            out_specs=pl.BlockSpec(memory_space=pl.ANY),
            scratch_shapes=[pltpu.VMEM((2, MOE_T, D), F32), pltpu.VMEM((MOE_T, D), BF16),
                            pltpu.VMEM((2, MOE_T, D), F32),
                            pltpu.SemaphoreType.DMA((2,)), pltpu.SemaphoreType.DMA((2,))]),
        compiler_params=_cparams(2),
        name="moe_experts",
    )(tile_e, tile_v, rowmap, h, w13, w13, w2)


def _rowmap_kernel(pos_ref, spare_ref, map_ref, sem):
    fill = pltpu.make_async_copy(spare_ref, map_ref, sem)
    fill.start()
    fill.wait()

    def token(t, c):
        map_ref[pos_ref[t]] = t
        map_ref[pos_ref[NTOK + t]] = NTOK + t
        return c

    lax.fori_loop(0, NTOK, token, 0, unroll=16)


def _rowmap(pos, spare):
    smem = pl.BlockSpec(memory_space=pltpu.SMEM)
    return pl.pallas_call(
        _rowmap_kernel,
        out_shape=jax.ShapeDtypeStruct((MAP_ROWS,), jnp.int32),
        in_specs=[smem, pl.BlockSpec(memory_space=pl.ANY)],
        out_specs=smem,
        scratch_shapes=[pltpu.SemaphoreType.DMA],
        name="moe_rowmap",
    )(pos, spare)


def _combine_kernel(y1_ref, y2_ref, meta_ref, x_ref, mod_ref, fg_ref, yp_ref, ys_ref):
    i = pl.program_id(0)
    meta = meta_ref[...]
    y = meta[:, 4:5] * y1_ref[...] + meta[:, 5:6] * y2_ref[...]
    x = x_ref[...] + mod_ref[5:6] * y
    out = x * lax.rsqrt(jnp.mean(x * x, axis=-1, keepdims=True) + EPS) * fg_ref[...]

    @pl.when(i < ROW_PT)
    def _():
        yp_ref[...] = out

    @pl.when(i >= ROW_PT)
    def _():
        ys_ref[...] = out


def _combine(y, meta, x, mods, final_g):
    slot2 = NTOK // ROW_T
    return pl.pallas_call(
        _combine_kernel,
        out_shape=(jax.ShapeDtypeStruct((NP_TOK, D), F32),
                   jax.ShapeDtypeStruct((NS_TOK, D), F32)),
        grid=(NTOK // ROW_T,),
        in_specs=[pl.BlockSpec((ROW_T, D), lambda i: (i, 0)),
                  pl.BlockSpec((ROW_T, D), lambda i: (slot2 + i, 0)),
                  pl.BlockSpec((ROW_T, LANES), lambda i: (i, 0)),
                  pl.BlockSpec((ROW_T, D), lambda i: (i, 0)),
                  _mod_spec(1, ROW_T),
                  _const_spec((1, D))],
        out_specs=(pl.BlockSpec((ROW_T, D), lambda i: (jnp.minimum(i, ROW_PT - 1), 0)),
                   pl.BlockSpec((ROW_T, D), lambda i: (jnp.maximum(i - ROW_PT, 0), 0))),
        compiler_params=_cparams(1),
        name="moe_combine",
    )(y, y, meta, x, mods, final_g)


def _routing_tables(meta, counts):
    cnt = counts[0, :N_EXP].astype(jnp.int32)
    padded = (cnt + MOE_T - 1) // MOE_T * MOE_T
    ends = jnp.cumsum(padded)
    off = ends - padded

    def expert_of_row(r):
        return sum((ends[e] <= r).astype(jnp.int32) for e in range(N_EXP - 1))

    def lookup(table, e):
        return sum(jnp.where(e == i, table[i], 0) for i in range(N_EXP))

    e1, e2, r1, r2 = (meta[:, i].astype(jnp.int32) for i in range(4))
    pos = MOE_T + jnp.concatenate([lookup(off, e1) + r1, lookup(off, e2) + r2])

    n_valid = ends[-1] // MOE_T
    tile = jnp.arange(MOE_STEPS, dtype=jnp.int32)
    tile_e = expert_of_row(jnp.minimum(tile, n_valid - 1) * MOE_T)
    tile_v = jnp.concatenate([jnp.zeros((1,), jnp.int32), (tile < n_valid).astype(jnp.int32)])

    row = jnp.arange(MAP_ROWS, dtype=jnp.int32) - MOE_T
    row_e = expert_of_row(row)
    pad_rank = jnp.clip(row - lookup(off + cnt, row_e), 0, MOE_T - 1)
    idle = Y_IDLE + (row & (MOE_T - 1))
    spare = jnp.where((row < 0) | (row >= ends[-1]), idle, 2 * NTOK + row_e * MOE_T + pad_rank)
    return tile_e, tile_v, pos, spare


def kernel(x_prompt, x_sample, cache_k_l1, cache_v_l1, c, c_ctx, ada_w, ada_b, norm_g, conv_w1, conv_b1, conv_dw, conv_dwb, conv_ln_g, conv_ln_b, conv_w2, attn_wqkv, attn_wo, attn_sink, ffn_w13, ffn_w2, moe_router, moe_w13, moe_w2, final_g):
    xp = x_prompt.reshape(NP_TOK, D)
    xs = x_sample.reshape(NS_TOK, D)
    cond = jnp.concatenate([c, c_ctx[None, :], jnp.zeros((N_COND - DEC_BATCH - 1, D), F32)])
    mods = _ada(cond, ada_w, ada_b)

    row = lambda a: a.reshape(1, -1)
    x1, (ffn_w13_b, ffn_w2_b, wqkv_b, wo_b, moe_w13_b) = _conv_layer(
        xp, xs, mods, row(norm_g[0, 0]), conv_w1.astype(BF16), row(conv_b1),
        conv_dw, row(conv_dwb), row(conv_ln_g), row(conv_ln_b), conv_w2.astype(BF16),
        [ffn_w13, ffn_w2, attn_wqkv, attn_wo, moe_w13.reshape(N_EXP * D, 2 * D_FFE)])
    x2, moe_w2_b = _ffn_layer(x1, mods, row(norm_g[0, 1]), ffn_w13_b, ffn_w2_b,
                              moe_w2.reshape(N_EXP * D_FFE, D))
    moe_w13_b = moe_w13_b.reshape(N_EXP, D, 2 * D_FFE)
    moe_w2_b = moe_w2_b.reshape(N_EXP, D_FFE, D)

    cos, sin = _rope_tables()
    q, k, v, state_k, state_v = _qkv_layer(x2, mods, row(norm_g[1, 0]), wqkv_b, cos, sin)
    o_ctx, o_lat = _attention(q, k, v, cache_k_l1.reshape(DEC_BATCH, PAST, KV_COLS),
                              cache_v_l1.reshape(DEC_BATCH, PAST, KV_COLS), attn_sink)

    x3, h, meta, counts = _wo_router(o_ctx, o_lat, x2, mods, row(norm_g[1, 1]), wo_b,
                                     moe_router.T)
    tile_e, tile_v, pos, spare = _routing_tables(meta, counts)
    rowmap = _rowmap(pos, spare)
    y = _experts(tile_e, tile_v, rowmap, h, moe_w13_b, moe_w2_b)
    y_p, y_s = _combine(y, meta, x3, mods, row(final_g))

    return (y_p.reshape(BATCH, SEQ, D), y_s.reshape(DEC_BATCH, DEC_SEQ, D),
            state_k.reshape(BATCH, SEQ, N_KV, HD), state_v.reshape(BATCH, SEQ, N_KV, HD))
```

```python
import functools

import jax
import jax.numpy as jnp
from jax import lax
from jax.experimental import pallas as pl
from jax.experimental.pallas import tpu as pltpu

F32 = jnp.float32
BF16 = jnp.bfloat16

D = 1024
BATCH, SEQ = 32, 256
DEC_BATCH, DEC_SEQ = 4, 2048
PAST = 256
NP_TOK = BATCH * SEQ
NS_TOK = DEC_BATCH * DEC_SEQ
NTOK = NP_TOK + NS_TOK
GRID_W = 64
N_HEADS, N_KV, HD = 16, 4, 64
GQA = N_HEADS // N_KV
KV_COLS = N_KV * HD
QKV_COLS = D + 2 * KV_COLS
WINDOW = 128
ROPE_THETA = 10000.0
CONV_W = 31
CONV_PAD = CONV_W // 2
D_FF = 2816
N_EXP = 8
D_FFE = 3584
N_MOD = 6
EPS = 1e-6
NEG_INF = -1e30
LOG2E = 1.4426950408889634

N_COND = 8
CTX_ROW = DEC_BATCH

CONV_T = 256
HALO = 16
ROW_T = 512
ATT_T = 256
ATT_WIN = ATT_T + 2 * WINDOW
FF_T = 1408
MOE_T = 512
MOE_FT = 1792
MOE_TILES = 2 * NTOK // MOE_T + N_EXP
MOE_ROWS = MOE_TILES * MOE_T
LANES = 128

VMEM_LIMIT = 56 * 1024 * 1024


def _cparams(n_axes=1, vmem=VMEM_LIMIT, flags=None):
    return pltpu.CompilerParams(dimension_semantics=("arbitrary",) * n_axes,
                                vmem_limit_bytes=vmem, flags=flags)


def _silu(x):
    return x * jax.nn.sigmoid(x)


def _norm_mod(x, g, shift, scale):
    y = x * lax.rsqrt(jnp.mean(x * x, axis=-1, keepdims=True) + EPS)
    return (y * g) * (1.0 + scale) + shift


def _cond_row(tile, tile_rows):
    n_prompt_tiles = NP_TOK // tile_rows
    per_seq = DEC_SEQ // tile_rows
    return jnp.where(tile < n_prompt_tiles, CTX_ROW, (tile - n_prompt_tiles) // per_seq)


def _mod_spec(layer, tile_rows):
    return pl.BlockSpec((None, None, N_MOD, D),
                        lambda i, *_: (layer, _cond_row(i, tile_rows), 0, 0))


def _const_spec(shape):
    nd = len(shape)
    return pl.BlockSpec(shape, lambda *_: (0,) * nd)


BF16_SUBLANES = 16


def _cast_specs(arrays, steps, step_of=lambda i: i):
    specs, shapes = [], []
    for a in arrays:
        rows, cols = a.shape
        n = steps
        while rows % n or (rows // n) % BF16_SUBLANES:
            n -= 1
        specs.append(pl.BlockSpec((rows // n, cols),
                                  lambda *g, n=n: (jnp.minimum(step_of(*g), n - 1), 0)))
        shapes.append(jax.ShapeDtypeStruct((rows, cols), BF16))
    return specs, shapes


def _cast_blocks(in_refs, out_refs):
    for w_ref, o_ref in zip(in_refs, out_refs):
        o_ref[...] = w_ref[...].astype(BF16)


ADA_TN = 1536


def _ada_kernel(cond_ref, w_ref, b_ref, o_ref):
    s = _silu(cond_ref[...]).astype(BF16)
    o_ref[...] = jnp.dot(s, w_ref[...].astype(BF16), preferred_element_type=F32) + b_ref[...]


def _ada(cond, ada_w, ada_b):
    depth = ada_w.shape[0]
    out = pl.pallas_call(
        _ada_kernel,
        out_shape=jax.ShapeDtypeStruct((depth, N_COND, N_MOD * D), F32),
        grid=(depth, N_MOD * D // ADA_TN),
        in_specs=[pl.BlockSpec((N_COND, D), lambda l, n: (0, 0)),
                  pl.BlockSpec((None, D, ADA_TN), lambda l, n: (l, 0, n)),
                  pl.BlockSpec((None, 1, ADA_TN), lambda l, n: (l, 0, n))],
        out_specs=pl.BlockSpec((None, N_COND, ADA_TN), lambda l, n: (l, 0, n)),
        compiler_params=_cparams(2),
        name="adaln",
    )(cond, ada_w, ada_b.reshape(depth, 1, N_MOD * D))
    return out.reshape(depth, N_COND, N_MOD, D)


CONV_PT = NP_TOK // CONV_T
CONV_TPS = DEC_SEQ // CONV_T
CONV_E = CONV_T + 2 * HALO
CONV_RB = 128


def _hold(piece):
    return jnp.where(piece != piece, piece, 0.0)


def _conv_stages(x_main, x_prev, x_next, tile, mod_ref, weights, g_scr, y_scr, store):
    ng_ref, w1_ref, b1_ref, dw_ref, dwb_ref, lng_ref, lnb_ref, w2_ref = weights
    j = tile - CONV_PT
    has_prev = jnp.logical_and(tile >= CONV_PT, j % CONV_TPS != 0)
    has_next = jnp.logical_and(tile >= CONV_PT, j % CONV_TPS != CONV_TPS - 1)
    shift, scale, gate = mod_ref[0:1], mod_ref[1:2], mod_ref[2:3]

    xe = jnp.concatenate([x_prev, x_main, x_next], axis=0)
    h = _norm_mod(xe, ng_ref[...], shift, scale).astype(BF16)
    uv = jnp.dot(h, w1_ref[...], preferred_element_type=F32) + b1_ref[...]
    g = uv[:, :D] * jax.nn.sigmoid(uv[:, D:])
    g_scr[0:HALO] = jnp.where(has_prev, g[0:HALO], 0.0)
    g_scr[HALO:HALO + CONV_T] = g[HALO:HALO + CONV_T]
    g_scr[HALO + CONV_T:] = jnp.where(has_next, g[HALO + CONV_T:], 0.0)
    tie = yield g[0:1, 0:1]

    base = HALO - CONV_PAD
    sub = 8
    for c in range(D // LANES):
        cs = slice(c * LANES, (c + 1) * LANES)
        for t0 in range(0, CONV_T, CONV_RB):
            y = dwb_ref[:, cs] if tie is None else dwb_ref[:, cs] + _hold(tie)
            for r in range(sub):
                z = None
                for a in range((base + CONV_W - 1) // sub + 1):
                    k = sub * a + r - base
                    if 0 <= k < CONV_W:
                        rows = slice(t0 + sub * a, t0 + sub * a + CONV_RB + sub)
                        term = g_scr[rows, cs] * dw_ref[k:k + 1, cs]
                        z = term if z is None else z + term
                y = y + z[r:r + CONV_RB]
            y_scr[t0:t0 + CONV_RB, cs] = y
        tie = yield y[0:1, 0:1]

    y = y_scr[...]
    mu = jnp.mean(y, axis=-1, keepdims=True)
    d = y - mu
    var = jnp.mean(d * d, axis=-1, keepdims=True)
    yn = d * lax.rsqrt(var + EPS) * lng_ref[...] + lnb_ref[...]
    o = jnp.dot(_silu(yn).astype(BF16), w2_ref[...], preferred_element_type=F32)
    store(x_main + gate * o)


def _ffn_stages(x_ref, mod_ref, ng_ref, w13_ref, w2_ref, store):
    x = x_ref[...]
    h = _norm_mod(x, ng_ref[...], mod_ref[3:4], mod_ref[4:5]).astype(BF16)

    def tied(v, tie):
        return v if tie is None else v + _hold(tie).astype(v.dtype)

    acc = jnp.zeros(x.shape, F32)
    piece = None
    for f in range(D_FF // FF_T):
        tie = yield piece
        u = jnp.dot(tied(h, tie), w13_ref[:, f * FF_T:(f + 1) * FF_T],
                    preferred_element_type=F32)
        tie = yield u[0:1, 0:1]
        g = jnp.dot(tied(h, tie), w13_ref[:, D_FF + f * FF_T:D_FF + (f + 1) * FF_T],
                    preferred_element_type=F32)
        tie = yield g[0:1, 0:1]
        a = tied(_silu(u) * g, tie).astype(BF16)
        acc = acc + jnp.dot(a, w2_ref[f * FF_T:(f + 1) * FF_T, :], preferred_element_type=F32)
        piece = acc[0:1, 0:1]
    yield piece
    store(x + mod_ref[5:6] * acc)


def _run(stages):
    for _ in stages:
        pass


def _zip_stages(vpu_stages, mxu_stages, plan):
    sides = {"v": vpu_stages, "m": mxu_stages}
    done = {"v": [next(vpu_stages)], "m": []}
    next(mxu_stages)
    for side, wait in plan:
        other = "m" if side == "v" else "v"
        try:
            done[side].append(sides[side].send(None if wait is None else done[other][wait]))
        except StopIteration:
            pass


L0_TILES = NTOK // CONV_T
L0_STEPS = L0_TILES // 2
L0_PLAN = (("v", None), ("v", None), ("m", 0), ("v", 0), ("m", None), ("v", 1), ("m", None),
           ("v", 2), ("v", 2), ("m", 4), ("v", 3), ("m", 6), ("v", 4), ("m", 8),
           ("v", None), ("m", None))


def _layer0_kernel(xp0_ref,
                   xpa_ref, xsa_ref, preva_ref, nexta_ref,
                   xpb_ref, xsb_ref, prevb_ref, nextb_ref,
                   mod0_ref, moda_ref, modb_ref,
                   ng0_ref, w1_ref, b1_ref, dw_ref, dwb_ref, lng_ref, lnb_ref, w2_ref,
                   ng1_ref, w13_ref, w2f_ref, cqkv_ref, cwo_ref,
                   o_ref, oqkv_ref, owo_ref,
                   buf_a, buf_b, g_scr, y_scr, g_scr_b, y_scr_b):
    s = pl.program_id(0)
    _cast_blocks([cqkv_ref, cwo_ref], [oqkv_ref, owo_ref])
    conv_w = (ng0_ref, w1_ref, b1_ref, dw_ref, dwb_ref, lng_ref, lnb_ref, w2_ref)
    ffn_w = (ng1_ref, w13_ref, w2f_ref)

    def put(ref, rows=slice(None)):
        def store(value):
            ref[rows] = value
        return store

    def conv(xp_ref, xs_ref, prev_ref, next_ref, tile, mod_ref, dst, scratch):
        x_main = jnp.where(tile < CONV_PT, xp_ref[...], xs_ref[...])
        return _conv_stages(x_main, prev_ref[...], next_ref[...], tile, mod_ref, conv_w,
                            *scratch, put(dst))

    @pl.when(s == 0)
    def _():
        _run(_conv_stages(xp0_ref[...], xp0_ref[0:HALO], xp0_ref[0:HALO], 0, mod0_ref,
                          conv_w, g_scr, y_scr, put(buf_a)))

    def ffn(x_ref, mod_ref, rows):
        return _ffn_stages(x_ref, mod_ref, *ffn_w, put(o_ref, rows))

    _zip_stages(conv(xpa_ref, xsa_ref, preva_ref, nexta_ref, 2 * s + 1, moda_ref, buf_b,
                     (g_scr, y_scr)),
                ffn(buf_a, mod0_ref, slice(0, CONV_T)), L0_PLAN)
    _zip_stages(conv(xpb_ref, xsb_ref, prevb_ref, nextb_ref,
                     jnp.minimum(2 * s + 2, L0_TILES - 1), modb_ref, buf_a,
                     (g_scr_b, y_scr_b)),
                ffn(buf_b, moda_ref, slice(CONV_T, 2 * CONV_T)), L0_PLAN)


def _layer0(xp, xs, mods, ng0, w1, b1, dw, dwb, lng, lnb, w2, ng1, w13, w2f, to_cast):
    hb = CONV_T // HALO
    n_hb = NS_TOK // HALO
    cast_specs, cast_shapes = _cast_specs(to_cast, L0_STEPS)

    def tile_specs(tile_of):
        def latent(s):
            return jnp.clip(tile_of(s) - CONV_PT, 0, L0_TILES - CONV_PT - 1)
        return [
            pl.BlockSpec((CONV_T, D), lambda s: (jnp.minimum(tile_of(s), CONV_PT - 1), 0)),
            pl.BlockSpec((CONV_T, D), lambda s: (latent(s), 0)),
            pl.BlockSpec((HALO, D), lambda s: (jnp.maximum(latent(s) * hb - 1, 0), 0)),
            pl.BlockSpec((HALO, D), lambda s: (jnp.minimum(latent(s) * hb + hb, n_hb - 1), 0)),
        ]

    def mod_spec(tile_of):
        return pl.BlockSpec((None, None, N_MOD, D),
                            lambda s: (0, _cond_row(tile_of(s), CONV_T), 0, 0))

    def resident(shape):
        return pl.BlockSpec(shape, lambda s: (0, 0), pipeline_mode=pl.Buffered(1))

    tile_0 = lambda s: 2 * s
    tile_a = lambda s: 2 * s + 1
    tile_b = lambda s: jnp.minimum(2 * s + 2, L0_TILES - 1)
    outs = pl.pallas_call(
        _layer0_kernel,
        out_shape=[jax.ShapeDtypeStruct((NTOK, D), F32)] + cast_shapes,
        grid=(L0_STEPS,),
        in_specs=[_const_spec((CONV_T, D))] + tile_specs(tile_a) + tile_specs(tile_b)
        + [mod_spec(tile_0), mod_spec(tile_a), mod_spec(tile_b),
           _const_spec((1, D)), resident((D, 2 * D)), _const_spec((1, 2 * D)),
           _const_spec((CONV_W, D)), _const_spec((1, D)), _const_spec((1, D)),
           _const_spec((1, D)), resident((D, D)),
           _const_spec((1, D)), resident((D, 2 * D_FF)), resident((D_FF, D))]
        + cast_specs,
        out_specs=[pl.BlockSpec((2 * CONV_T, D), lambda s: (s, 0))] + cast_specs,
        scratch_shapes=[pltpu.VMEM((CONV_T, D), F32), pltpu.VMEM((CONV_T, D), F32),
                        pltpu.VMEM((CONV_E, D), F32), pltpu.VMEM((CONV_T, D), F32),
                        pltpu.VMEM((CONV_E, D), F32), pltpu.VMEM((CONV_T, D), F32)],
        compiler_params=_cparams(1),
        name="layer0",
    )(xp, xp, xs, xs, xs, xp, xs, xs, xs, mods, mods, mods,
      ng0, w1, b1, dw, dwb, lng, lnb, w2, ng1, w13, w2f, *to_cast)
    return outs[0], outs[1:]


ROW_PT = NP_TOK // ROW_T
ROW_TPS = DEC_SEQ // ROW_T


def _rope_tables():
    half = HD // 2
    inv = ROPE_THETA ** (-jnp.arange(0, half, 2, dtype=F32) / half)
    t = jnp.arange(DEC_SEQ)
    lane = jnp.arange(LANES)
    jj = lane % HD
    pos = jnp.where(jj[None, :] < half, (t // GRID_W)[:, None], (t % GRID_W)[:, None]).astype(F32)
    ang = pos * inv[jj % (half // 2)][None, :]
    first = (jj % half) < (half // 2)
    cos = jnp.cos(ang)
    sin = jnp.where(first[None, :], -jnp.sin(ang), jnp.sin(ang))
    cos = jnp.concatenate([cos.reshape(ROW_TPS, ROW_T, LANES), jnp.ones((1, ROW_T, LANES), F32)])
    sin = jnp.concatenate([sin.reshape(ROW_TPS, ROW_T, LANES), jnp.zeros((1, ROW_T, LANES), F32)])
    return cos, sin


def _rope(x, cos, sin):
    lane = lax.broadcasted_iota(jnp.int32, x.shape, 1)
    first = (lane % (HD // 2)) < (HD // 4)
    swapped = jnp.where(first, pltpu.roll(x, LANES - HD // 4, axis=1), pltpu.roll(x, HD // 4, axis=1))
    return x * cos + swapped * sin


def _qkv_kernel(x_ref, mod_ref, ng_ref, w_ref, cos_ref, sin_ref,
                q_ref, k_ref, v_ref, sk_ref, sv_ref):
    i = pl.program_id(0)
    h = _norm_mod(x_ref[...], ng_ref[...], mod_ref[0:1], mod_ref[1:2]).astype(BF16)
    y = jnp.dot(h, w_ref[...], preferred_element_type=F32)
    k = y[:, D:D + KV_COLS]
    v = y[:, D + KV_COLS:]

    @pl.when(i < ROW_PT)
    def _():
        sk_ref[...] = k
        sv_ref[...] = v

    cos, sin = cos_ref[...], sin_ref[...]
    scale = HD ** -0.5 * LOG2E
    for c in range(D // LANES):
        cs = slice(c * LANES, (c + 1) * LANES)
        q_ref[:, cs] = (_rope(y[:, cs], cos, sin) * scale).astype(BF16)
    for c in range(KV_COLS // LANES):
        cs = slice(c * LANES, (c + 1) * LANES)
        k_ref[:, cs] = _rope(k[:, cs], cos, sin).astype(BF16)
    v_ref[...] = v.astype(BF16)


def _qkv_layer(x, mods, ng, wqkv, cos, sin):
    def tab(i):
        return (jnp.where(i < ROW_PT, ROW_TPS, (i - ROW_PT) % ROW_TPS), 0, 0)

    def state(i):
        return (jnp.minimum(i, ROW_PT - 1), 0)

    return pl.pallas_call(
        _qkv_kernel,
        out_shape=(jax.ShapeDtypeStruct((NTOK, D), BF16),
                   jax.ShapeDtypeStruct((NTOK, KV_COLS), BF16),
                   jax.ShapeDtypeStruct((NTOK, KV_COLS), BF16),
                   jax.ShapeDtypeStruct((NP_TOK, KV_COLS), F32),
                   jax.ShapeDtypeStruct((NP_TOK, KV_COLS), F32)),
        grid=(NTOK // ROW_T,),
        in_specs=[pl.BlockSpec((ROW_T, D), lambda i: (i, 0)),
                  _mod_spec(1, ROW_T),
                  _const_spec((1, D)),
                  _const_spec((D, QKV_COLS)),
                  pl.BlockSpec((None, ROW_T, LANES), tab),
                  pl.BlockSpec((None, ROW_T, LANES), tab)],
        out_specs=(pl.BlockSpec((ROW_T, D), lambda i: (i, 0)),
                   pl.BlockSpec((ROW_T, KV_COLS), lambda i: (i, 0)),
                   pl.BlockSpec((ROW_T, KV_COLS), lambda i: (i, 0)),
                   pl.BlockSpec((ROW_T, KV_COLS), state),
                   pl.BlockSpec((ROW_T, KV_COLS), state)),
        compiler_params=_cparams(1),
        name="qkv_rope",
    )(x, mods, ng, wqkv, cos, sin)


def _attn_heads(q, parts, sink_ref, o_ref):
    nt = (((1,), (1,)), ((), ()))
    tq = q.shape[0]
    for h in range(N_HEADS):
        kv = h // GQA
        hs = slice(kv * HD, (kv + 1) * HD)
        qh = q[:, h * HD:(h + 1) * HD]
        sink = sink_ref[h] * LOG2E
        scores = []
        m = jnp.full((tq, 1), sink, F32)
        for k, _, bias in parts:
            s = lax.dot_general(qh, k[:, hs], nt, preferred_element_type=F32)
            if bias is not None:
                s = s + bias
            scores.append(s)
            m = jnp.maximum(m, jnp.max(s, axis=-1, keepdims=True))
        denom = jnp.exp2(sink - m)
        o = jnp.zeros((tq, HD), F32)
        for s, (_, v, _) in zip(scores, parts):
            p = jnp.exp2(s - m)
            denom = denom + jnp.sum(p, axis=-1, keepdims=True)
            o = o + jnp.dot(p.astype(BF16), v[:, hs], preferred_element_type=F32)
        o_ref[:, h * HD:(h + 1) * HD] = (o / denom).astype(o_ref.dtype)


def _ctx_attn_kernel(sink_ref, q_ref, k_ref, v_ref, cast_ref, o_ref, cast_out_ref):
    _cast_blocks([cast_ref], [cast_out_ref])
    _attn_heads(q_ref[...], [(k_ref[...], v_ref[...], None)], sink_ref, o_ref)


def _lat_attn_kernel(sink_ref, q_ref, k_ref, v_ref, ck_ref, cv_ref, cast_ref,
                     o_ref, cast_out_ref):
    _cast_blocks([cast_ref], [cast_out_ref])
    j = pl.program_id(1)
    w0 = pl.multiple_of(jnp.clip(j * ATT_T - WINDOW, 0, DEC_SEQ - ATT_WIN), WINDOW)
    qpos = j * ATT_T + lax.broadcasted_iota(jnp.int32, (ATT_T, ATT_WIN), 0)
    kpos = w0 + lax.broadcasted_iota(jnp.int32, (ATT_T, ATT_WIN), 1)
    bias = jnp.where(jnp.abs(qpos - kpos) <= WINDOW, 0.0, NEG_INF)
    parts = [(k_ref[pl.ds(w0, ATT_WIN), :], v_ref[pl.ds(w0, ATT_WIN), :], bias),
             (ck_ref[...].astype(BF16), cv_ref[...].astype(BF16), None)]
    _attn_heads(q_ref[...], parts, sink_ref, o_ref)


def _attention(q, k, v, ck, cv, sink, cast_ctx, cast_lat):
    smem = pl.BlockSpec(memory_space=pltpu.SMEM)
    cast_specs, cast_shapes = _cast_specs([cast_ctx], BATCH)
    o_ctx, ctx_b = pl.pallas_call(
        _ctx_attn_kernel,
        out_shape=[jax.ShapeDtypeStruct((NP_TOK, D), BF16)] + cast_shapes,
        grid=(BATCH,),
        in_specs=[smem,
                  pl.BlockSpec((SEQ, D), lambda b: (b, 0)),
                  pl.BlockSpec((SEQ, KV_COLS), lambda b: (b, 0)),
                  pl.BlockSpec((SEQ, KV_COLS), lambda b: (b, 0))] + cast_specs,
        out_specs=[pl.BlockSpec((SEQ, D), lambda b: (b, 0))] + cast_specs,
        compiler_params=_cparams(1),
        name="ctx_attention",
    )(sink, q, k, v, cast_ctx)

    q_off = NP_TOK // ATT_T
    kv_off = NP_TOK // DEC_SEQ
    tps = DEC_SEQ // ATT_T
    cast_specs, cast_shapes = _cast_specs([cast_lat], DEC_BATCH * tps, lambda b, j: b * tps + j)
    o_lat, lat_b = pl.pallas_call(
        _lat_attn_kernel,
        out_shape=[jax.ShapeDtypeStruct((NS_TOK, D), BF16)] + cast_shapes,
        grid=(DEC_BATCH, tps),
        in_specs=[smem,
                  pl.BlockSpec((ATT_T, D), lambda b, j: (q_off + b * tps + j, 0)),
                  pl.BlockSpec((DEC_SEQ, KV_COLS), lambda b, j: (kv_off + b, 0)),
                  pl.BlockSpec((DEC_SEQ, KV_COLS), lambda b, j: (kv_off + b, 0)),
                  pl.BlockSpec((None, PAST, KV_COLS), lambda b, j: (b, 0, 0)),
                  pl.BlockSpec((None, PAST, KV_COLS), lambda b, j: (b, 0, 0))] + cast_specs,
        out_specs=[pl.BlockSpec((ATT_T, D), lambda b, j: (b * tps + j, 0))] + cast_specs,
        compiler_params=_cparams(2),
        name="latent_attention",
    )(sink, q, k, v, ck, cv, cast_lat)
    return o_ctx, o_lat, ctx_b, lat_b


def _wo_router_kernel(oc_ref, ol_ref, x_ref, mod_ref, ng_ref, wo_ref, r_ref,
                      x3_ref, h_ref, meta_ref, cnt_ref, carry):
    i = pl.program_id(0)

    @pl.when(i == 0)
    def _():
        carry[...] = jnp.zeros_like(carry)

    a = jnp.where(i < ROW_PT, oc_ref[...], ol_ref[...])
    o = jnp.dot(a, wo_ref[...], preferred_element_type=F32)
    x3 = x_ref[...] + mod_ref[2:3] * o
    x3_ref[...] = x3
    h = _norm_mod(x3, ng_ref[...], mod_ref[3:4], mod_ref[4:5])
    h_ref[...] = h

    logit = [jnp.sum(h * r_ref[e:e + 1, :], axis=-1, keepdims=True) for e in range(N_EXP)]

    def top1(cols):
        m = functools.reduce(jnp.maximum, cols)
        idx = jnp.full_like(m, float(N_EXP))
        for e in reversed(range(N_EXP)):
            idx = jnp.where(cols[e] == m, float(e), idx)
        return m, idx

    m1, e1 = top1(logit)
    m2, e2 = top1([jnp.where(e1 == float(e), -jnp.inf, logit[e]) for e in range(N_EXP)])
    lane = lax.broadcasted_iota(jnp.int32, (ROW_T, LANES), 1).astype(F32)
    t = jnp.exp(m2 - m1)
    g1 = 1.0 / (1.0 + t)
    g2 = t / (1.0 + t)

    oh1 = (lane == e1).astype(F32)
    oh2 = (lane == e2).astype(F32)
    oh = oh1 + oh2
    r_i = lax.broadcasted_iota(jnp.int32, (ROW_T, ROW_T), 0)
    c_i = lax.broadcasted_iota(jnp.int32, (ROW_T, ROW_T), 1)
    tri = (c_i < r_i).astype(BF16)
    before = jnp.dot(tri, oh.astype(BF16), preferred_element_type=F32) + carry[0:1, :]
    rank1 = jnp.sum(before * oh1, axis=-1, keepdims=True)
    rank2 = jnp.sum(before * oh2, axis=-1, keepdims=True)
    carry[0:1, :] = carry[0:1, :] + jnp.sum(oh, axis=0, keepdims=True)
    cnt_ref[...] = carry[...]

    meta = jnp.where(lane == 0, e1, 0.0)
    meta = jnp.where(lane == 1, e2, meta)
    meta = jnp.where(lane == 2, rank1, meta)
    meta = jnp.where(lane == 3, rank2, meta)
    meta = jnp.where(lane == 4, g1, meta)
    meta = jnp.where(lane == 5, g2, meta)
    meta_ref[...] = meta


def _wo_router(o_ctx, o_lat, x, mods, ng, wo, router_t):
    return pl.pallas_call(
        _wo_router_kernel,
        out_shape=(jax.ShapeDtypeStruct((NTOK, D), F32),
                   jax.ShapeDtypeStruct((NTOK, D), F32),
                   jax.ShapeDtypeStruct((NTOK, LANES), F32),
                   jax.ShapeDtypeStruct((8, LANES), F32)),
        grid=(NTOK // ROW_T,),
        in_specs=[pl.BlockSpec((ROW_T, D), lambda i: (jnp.minimum(i, ROW_PT - 1), 0)),
                  pl.BlockSpec((ROW_T, D), lambda i: (jnp.maximum(i - ROW_PT, 0), 0)),
                  pl.BlockSpec((ROW_T, D), lambda i: (i, 0)),
                  _mod_spec(1, ROW_T),
                  _const_spec((1, D)),
                  _const_spec((D, D)),
                  _const_spec((N_EXP, D))],
        out_specs=(pl.BlockSpec((ROW_T, D), lambda i: (i, 0)),
                   pl.BlockSpec((ROW_T, D), lambda i: (i, 0)),
                   pl.BlockSpec((ROW_T, LANES), lambda i: (i, 0)),
                   pl.BlockSpec((8, LANES), lambda i: (0, 0))),
        scratch_shapes=[pltpu.VMEM((8, LANES), F32)],
        compiler_params=_cparams(1),
        name="wo_router",
    )(o_ctx, o_lat, x, mods, ng, wo, router_t)


MOE_NF = D_FFE // MOE_FT
assert MOE_NF == 2
MOE_STEPS = MOE_TILES + 1
Y_IDLE = 2 * NTOK + N_EXP * MOE_T
Y_ROWS = Y_IDLE + MOE_T
MAP_ROWS = (MOE_TILES + 2) * MOE_T
assert NTOK & (NTOK - 1) == 0


def _expert_kernel(te_ref, tv_ref, map_ref, h_ref, wu_ref, wg_ref, w2_ref, y_ref,
                   xbuf, xb, obuf, gsem, ssem):
    j = pl.program_id(0)
    f = pl.program_id(1)
    cur = j % 2
    nxt = 1 - cur
    computes = tv_ref[j + 1] == 1
    drains = jnp.logical_and(tv_ref[j] == 1, jnp.logical_not(computes))

    def gather(tile, slot, r):
        src = map_ref[(tile + 1) * MOE_T + r] & (NTOK - 1)
        return pltpu.make_async_copy(h_ref.at[pl.ds(src, 1)], xbuf.at[slot, pl.ds(r, 1)],
                                     gsem.at[slot])

    def gather_all(slot):
        return pltpu.make_async_copy(h_ref.at[pl.ds(0, MOE_T)], xbuf.at[slot], gsem.at[slot])

    def scatter(tile, slot, r):
        dst = map_ref[(tile + 1) * MOE_T + r]
        return pltpu.make_async_copy(obuf.at[slot, pl.ds(r, 1)], y_ref.at[pl.ds(dst, 1)],
                                     ssem.at[slot])

    def scatter_all(slot):
        return pltpu.make_async_copy(obuf.at[slot], y_ref.at[pl.ds(0, MOE_T)], ssem.at[slot])

    @pl.when(jnp.logical_and(j == 0, f == 0))
    def _():
        obuf[...] = jnp.zeros_like(obuf)
        for s in range(N_EXP):
            spare = pltpu.make_async_copy(
                obuf.at[0], y_ref.at[pl.ds(2 * NTOK + s * MOE_T, MOE_T)], ssem.at[0])
            spare.start()
            spare.wait()

        def first(r, c):
            gather(0, 0, r).start()
            scatter(-1, 0, r).start()
            return c

        lax.fori_loop(0, MOE_T, first, 0)

    @pl.when(jnp.logical_and(computes, f == 0))
    def _():
        gather_all(cur).wait()
        xb[...] = xbuf[cur].astype(BF16)
        for r in range(MOE_T):
            gather(j + 1, nxt, r).start()
        x = xb[...]
        u = jnp.dot(x, wu_ref[...], preferred_element_type=F32)
        g = jnp.dot(x, wg_ref[...], preferred_element_type=F32)
        a = (_silu(u) * g).astype(BF16)
        y = jnp.dot(a, w2_ref[...], preferred_element_type=F32)
        scatter_all(cur).wait()
        obuf[cur] = y

    @pl.when(jnp.logical_and(computes, f == 1))
    def _():
        for r in range(MOE_T):
            scatter(j - 1, nxt, r).start()
        x = xb[...]
        u = jnp.dot(x, wu_ref[...], preferred_element_type=F32)
        g = jnp.dot(x, wg_ref[...], preferred_element_type=F32)
        a = (_silu(u) * g).astype(BF16)
        y = jnp.dot(a, w2_ref[...], preferred_element_type=F32)
        obuf[cur] = obuf[cur] + y

    @pl.when(jnp.logical_and(drains, f == 0))
    def _():
        gather_all(cur).wait()
        scatter_all(cur).wait()

    @pl.when(jnp.logical_and(drains, f == 1))
    def _():
        def last(r, c):
            scatter(j - 1, nxt, r).start()
            return c

        lax.fori_loop(0, MOE_T, last, 0)
        scatter_all(nxt).wait()


def _experts(tile_e, tile_v, rowmap, h, w13, w2):
    def fidx(j, f, tv):
        return jnp.where(tv[j + 1] == 1, f, MOE_NF - 1)

    return pl.pallas_call(
        _expert_kernel,
        out_shape=jax.ShapeDtypeStruct((Y_ROWS, D), F32),
        grid_spec=pltpu.PrefetchScalarGridSpec(
            num_scalar_prefetch=3,
            grid=(MOE_STEPS, MOE_NF),
            in_specs=[
                pl.BlockSpec(memory_space=pl.ANY),
                pl.BlockSpec((None, D, MOE_FT), lambda j, f, te, tv, *_: (te[j], 0, fidx(j, f, tv))),
                pl.BlockSpec((None, D, MOE_FT),
                             lambda j, f, te, tv, *_: (te[j], 0, MOE_NF + fidx(j, f, tv))),
                pl.BlockSpec((None, MOE_FT, D), lambda j, f, te, tv, *_: (te[j], fidx(j, f, tv), 0)),
            ],
            out_specs=pl.BlockSpec(memory_space=pl.ANY),
            scratch_shapes=[pltpu.VMEM((2, MOE_T, D), F32), pltpu.VMEM((MOE_T, D), BF16),
                            pltpu.VMEM((2, MOE_T, D), F32),
                            pltpu.SemaphoreType.DMA((2,)), pltpu.SemaphoreType.DMA((2,))]),
        compiler_params=_cparams(2),
        name="moe_experts",
    )(tile_e, tile_v, rowmap, h, w13, w13, w2)


def _rowmap_kernel(pos_ref, spare_ref, map_ref, sem):
    fill = pltpu.make_async_copy(spare_ref, map_ref, sem)
    fill.start()
    fill.wait()

    def token(t, c):
        map_ref[pos_ref[t]] = t
        map_ref[pos_ref[NTOK + t]] = NTOK + t
        return c

    lax.fori_loop(0, NTOK, token, 0, unroll=16)


def _rowmap(pos, spare):
    smem = pl.BlockSpec(memory_space=pltpu.SMEM)
    return pl.pallas_call(
        _rowmap_kernel,
        out_shape=jax.ShapeDtypeStruct((MAP_ROWS,), jnp.int32),
        in_specs=[smem, pl.BlockSpec(memory_space=pl.ANY)],
        out_specs=smem,
        scratch_shapes=[pltpu.SemaphoreType.DMA],
        name="moe_rowmap",
    )(pos, spare)


def _combine_kernel(y1_ref, y2_ref, meta_ref, x_ref, mod_ref, fg_ref, yp_ref, ys_ref):
    i = pl.program_id(0)
    meta = meta_ref[...]
    y = meta[:, 4:5] * y1_ref[...] + meta[:, 5:6] * y2_ref[...]
    x = x_ref[...] + mod_ref[5:6] * y
    out = x * lax.rsqrt(jnp.mean(x * x, axis=-1, keepdims=True) + EPS) * fg_ref[...]

    @pl.when(i < ROW_PT)
    def _():
        yp_ref[...] = out

    @pl.when(i >= ROW_PT)
    def _():
        ys_ref[...] = out


def _combine(y, meta, x, mods, final_g):
    slot2 = NTOK // ROW_T
    return pl.pallas_call(
        _combine_kernel,
        out_shape=(jax.ShapeDtypeStruct((NP_TOK, D), F32),
                   jax.ShapeDtypeStruct((NS_TOK, D), F32)),
        grid=(NTOK // ROW_T,),
        in_specs=[pl.BlockSpec((ROW_T, D), lambda i: (i, 0)),
                  pl.BlockSpec((ROW_T, D), lambda i: (slot2 + i, 0)),
                  pl.BlockSpec((ROW_T, LANES), lambda i: (i, 0)),
                  pl.BlockSpec((ROW_T, D), lambda i: (i, 0)),
                  _mod_spec(1, ROW_T),
                  _const_spec((1, D))],
        out_specs=(pl.BlockSpec((ROW_T, D), lambda i: (jnp.minimum(i, ROW_PT - 1), 0)),
                   pl.BlockSpec((ROW_T, D), lambda i: (jnp.maximum(i - ROW_PT, 0), 0))),
        compiler_params=_cparams(1),
        name="moe_combine",
    )(y, y, meta, x, mods, final_g)


def _routing_tables(meta, counts):
    cnt = counts[0, :N_EXP].astype(jnp.int32)
    padded = (cnt + MOE_T - 1) // MOE_T * MOE_T
    ends = jnp.cumsum(padded)
    off = ends - padded

    def expert_of_row(r):
        return sum((ends[e] <= r).astype(jnp.int32) for e in range(N_EXP - 1))

    def lookup(table, e):
        return sum(jnp.where(e == i, table[i], 0) for i in range(N_EXP))

    e1, e2, r1, r2 = (meta[:, i].astype(jnp.int32) for i in range(4))
    pos = MOE_T + jnp.concatenate([lookup(off, e1) + r1, lookup(off, e2) + r2])

    n_valid = ends[-1] // MOE_T
    tile = jnp.arange(MOE_STEPS, dtype=jnp.int32)
    tile_e = expert_of_row(jnp.minimum(tile, n_valid - 1) * MOE_T)
    tile_v = jnp.concatenate([jnp.zeros((1,), jnp.int32), (tile < n_valid).astype(jnp.int32)])

    row = jnp.arange(MAP_ROWS, dtype=jnp.int32) - MOE_T
    row_e = expert_of_row(row)
    pad_rank = jnp.clip(row - lookup(off + cnt, row_e), 0, MOE_T - 1)
    idle = Y_IDLE + (row & (MOE_T - 1))
    spare = jnp.where((row < 0) | (row >= ends[-1]), idle, 2 * NTOK + row_e * MOE_T + pad_rank)
    return tile_e, tile_v, pos, spare


def kernel(x_prompt, x_sample, cache_k_l1, cache_v_l1, c, c_ctx, ada_w, ada_b, norm_g, conv_w1, conv_b1, conv_dw, conv_dwb, conv_ln_g, conv_ln_b, conv_w2, attn_wqkv, attn_wo, attn_sink, ffn_w13, ffn_w2, moe_router, moe_w13, moe_w2, final_g):
    xp = x_prompt.reshape(NP_TOK, D)
    xs = x_sample.reshape(NS_TOK, D)
    cond = jnp.concatenate([c, c_ctx[None, :], jnp.zeros((N_COND - DEC_BATCH - 1, D), F32)])
    mods = _ada(cond, ada_w, ada_b)

    row = lambda a: a.reshape(1, -1)
    x2, (wqkv_b, wo_b) = _layer0(
        xp, xs, mods, row(norm_g[0, 0]), conv_w1.astype(BF16), row(conv_b1),
        conv_dw, row(conv_dwb), row(conv_ln_g), row(conv_ln_b), conv_w2.astype(BF16),
        row(norm_g[0, 1]), ffn_w13.astype(BF16), ffn_w2.astype(BF16), [attn_wqkv, attn_wo])

    cos, sin = _rope_tables()
    q, k, v, state_k, state_v = _qkv_layer(x2, mods, row(norm_g[1, 0]), wqkv_b, cos, sin)
    o_ctx, o_lat, moe_w2_b, moe_w13_b = _attention(
        q, k, v, cache_k_l1.reshape(DEC_BATCH, PAST, KV_COLS),
        cache_v_l1.reshape(DEC_BATCH, PAST, KV_COLS), attn_sink,
        moe_w2.reshape(N_EXP * D_FFE, D), moe_w13.reshape(N_EXP * D, 2 * D_FFE))
    moe_w13_b = moe_w13_b.reshape(N_EXP, D, 2 * D_FFE)
    moe_w2_b = moe_w2_b.reshape(N_EXP, D_FFE, D)

    x3, h, meta, counts = _wo_router(o_ctx, o_lat, x2, mods, row(norm_g[1, 1]), wo_b,
                                     moe_router.T)
    tile_e, tile_v, pos, spare = _routing_tables(meta, counts)
    rowmap = _rowmap(pos, spare)
    y = _experts(tile_e, tile_v, rowmap, h, moe_w13_b, moe_w2_b)
    y_p, y_s = _combine(y, meta, x3, mods, row(final_g))

    return (y_p.reshape(BATCH, SEQ, D), y_s.reshape(DEC_BATCH, DEC_SEQ, D),
            state_k.reshape(BATCH, SEQ, N_KV, HD), state_v.reshape(BATCH, SEQ, N_KV, HD))
```

```python
import functools

import jax
import jax.numpy as jnp
from jax import lax
from jax.experimental import pallas as pl
from jax.experimental.pallas import tpu as pltpu

F32 = jnp.float32
BF16 = jnp.bfloat16

D = 1024
BATCH, SEQ = 32, 256
DEC_BATCH, DEC_SEQ = 4, 2048
PAST = 256
NP_TOK = BATCH * SEQ
NS_TOK = DEC_BATCH * DEC_SEQ
NTOK = NP_TOK + NS_TOK
GRID_W = 64
N_HEADS, N_KV, HD = 16, 4, 64
GQA = N_HEADS // N_KV
KV_COLS = N_KV * HD
QKV_COLS = D + 2 * KV_COLS
WINDOW = 128
ROPE_THETA = 10000.0
CONV_W = 31
CONV_PAD = CONV_W // 2
D_FF = 2816
N_EXP = 8
D_FFE = 3584
N_MOD = 6
EPS = 1e-6
NEG_INF = -1e30
LOG2E = 1.4426950408889634

N_COND = 8
CTX_ROW = DEC_BATCH

CONV_T = 256
HALO = 16
ROW_T = 512
ATT_T = 256
ATT_WIN = ATT_T + 2 * WINDOW
FF_T = 1408
MOE_T = 512
MOE_FT = 1792
MOE_TILES = 2 * NTOK // MOE_T + N_EXP
MOE_ROWS = MOE_TILES * MOE_T
LANES = 128
SUBLANES = 8

VMEM_LIMIT = 56 * 1024 * 1024


def _cparams(n_axes=1, vmem=VMEM_LIMIT, flags=None):
    return pltpu.CompilerParams(dimension_semantics=("arbitrary",) * n_axes,
                                vmem_limit_bytes=vmem, flags=flags)


def _silu(x):
    return x * jax.nn.sigmoid(x)


def _norm_mod(x, g, shift, scale):
    y = x * lax.rsqrt(jnp.mean(x * x, axis=-1, keepdims=True) + EPS)
    return (y * g) * (1.0 + scale) + shift


def _cond_row(tile, tile_rows):
    n_prompt_tiles = NP_TOK // tile_rows
    per_seq = DEC_SEQ // tile_rows
    return jnp.where(tile < n_prompt_tiles, CTX_ROW, (tile - n_prompt_tiles) // per_seq)


def _mod_spec(layer, tile_rows):
    return pl.BlockSpec((None, None, N_MOD, D),
                        lambda i, *_: (layer, _cond_row(i, tile_rows), 0, 0))


def _const_spec(shape):
    nd = len(shape)
    return pl.BlockSpec(shape, lambda *_: (0,) * nd)


BF16_SUBLANES = 16


def _cast_specs(arrays, steps, step_of=lambda i: i):
    specs, shapes = [], []
    for a in arrays:
        rows, cols = a.shape
        n = steps
        while rows % n or (rows // n) % BF16_SUBLANES:
            n -= 1
        specs.append(pl.BlockSpec((rows // n, cols),
                                  lambda *g, n=n: (jnp.minimum(step_of(*g), n - 1), 0)))
        shapes.append(jax.ShapeDtypeStruct((rows, cols), BF16))
    return specs, shapes


def _cast_blocks(in_refs, out_refs):
    for w_ref, o_ref in zip(in_refs, out_refs):
        o_ref[...] = w_ref[...].astype(BF16)


ADA_TN = 1536


def _ada_kernel(cond_ref, w_ref, b_ref, o_ref):
    s = _silu(cond_ref[...]).astype(BF16)
    o_ref[...] = jnp.dot(s, w_ref[...].astype(BF16), preferred_element_type=F32) + b_ref[...]


def _ada(cond, ada_w, ada_b):
    depth = ada_w.shape[0]
    out = pl.pallas_call(
        _ada_kernel,
        out_shape=jax.ShapeDtypeStruct((depth, N_COND, N_MOD * D), F32),
        grid=(depth, N_MOD * D // ADA_TN),
        in_specs=[pl.BlockSpec((N_COND, D), lambda l, n: (0, 0)),
                  pl.BlockSpec((None, D, ADA_TN), lambda l, n: (l, 0, n)),
                  pl.BlockSpec((None, 1, ADA_TN), lambda l, n: (l, 0, n))],
        out_specs=pl.BlockSpec((None, N_COND, ADA_TN), lambda l, n: (l, 0, n)),
        compiler_params=_cparams(2),
        name="adaln",
    )(cond, ada_w, ada_b.reshape(depth, 1, N_MOD * D))
    return out.reshape(depth, N_COND, N_MOD, D)


CONV_PT = NP_TOK // CONV_T
CONV_TPS = DEC_SEQ // CONV_T
CONV_E = CONV_T + 2 * HALO
CONV_RB = 128


def _hold(piece):
    return jnp.where(piece != piece, piece, 0.0)


def _conv_stages(x_main, x_prev, x_next, tile, mod_ref, weights, g_scr, y_scr, store):
    ng_ref, w1_ref, b1_ref, dw_ref, dwb_ref, lng_ref, lnb_ref, w2_ref = weights
    j = tile - CONV_PT
    has_prev = jnp.logical_and(tile >= CONV_PT, j % CONV_TPS != 0)
    has_next = jnp.logical_and(tile >= CONV_PT, j % CONV_TPS != CONV_TPS - 1)
    shift, scale, gate = mod_ref[0:1], mod_ref[1:2], mod_ref[2:3]

    xe = jnp.concatenate([x_prev, x_main, x_next], axis=0)
    h = _norm_mod(xe, ng_ref[...], shift, scale).astype(BF16)
    uv = jnp.dot(h, w1_ref[...], preferred_element_type=F32) + b1_ref[...]
    g = uv[:, :D] * jax.nn.sigmoid(uv[:, D:])
    g_scr[0:HALO] = jnp.where(has_prev, g[0:HALO], 0.0)
    g_scr[HALO:HALO + CONV_T] = g[HALO:HALO + CONV_T]
    g_scr[HALO + CONV_T:] = jnp.where(has_next, g[HALO + CONV_T:], 0.0)
    tie = yield g[0:1, 0:1]

    base = HALO - CONV_PAD
    sub = SUBLANES
    for c in range(D // LANES):
        cs = slice(c * LANES, (c + 1) * LANES)
        for t0 in range(0, CONV_T, CONV_RB):
            y = dwb_ref[:, cs] if tie is None else dwb_ref[:, cs] + _hold(tie)
            for r in range(sub):
                z = None
                for a in range((base + CONV_W - 1) // sub + 1):
                    k = sub * a + r - base
                    if 0 <= k < CONV_W:
                        rows = slice(t0 + sub * a, t0 + sub * a + CONV_RB + sub)
                        term = g_scr[rows, cs] * dw_ref[k:k + 1, cs]
                        z = term if z is None else z + term
                y = y + z[r:r + CONV_RB]
            y_scr[t0:t0 + CONV_RB, cs] = y
        tie = yield y[0:1, 0:1]

    y = y_scr[...]
    mu = jnp.mean(y, axis=-1, keepdims=True)
    d = y - mu
    var = jnp.mean(d * d, axis=-1, keepdims=True)
    yn = d * lax.rsqrt(var + EPS) * lng_ref[...] + lnb_ref[...]
    o = jnp.dot(_silu(yn).astype(BF16), w2_ref[...], preferred_element_type=F32)
    store(x_main + gate * o)


def _ffn_stages(x_ref, mod_ref, ng_ref, w13_ref, w2_ref, store):
    x = x_ref[...]
    h = _norm_mod(x, ng_ref[...], mod_ref[3:4], mod_ref[4:5]).astype(BF16)

    def tied(v, tie):
        return v if tie is None else v + _hold(tie).astype(v.dtype)

    acc = jnp.zeros(x.shape, F32)
    piece = None
    for f in range(D_FF // FF_T):
        tie = yield piece
        u = jnp.dot(tied(h, tie), w13_ref[:, f * FF_T:(f + 1) * FF_T],
                    preferred_element_type=F32)
        tie = yield u[0:1, 0:1]
        g = jnp.dot(tied(h, tie), w13_ref[:, D_FF + f * FF_T:D_FF + (f + 1) * FF_T],
                    preferred_element_type=F32)
        tie = yield g[0:1, 0:1]
        a = tied(_silu(u) * g, tie).astype(BF16)
        acc = acc + jnp.dot(a, w2_ref[f * FF_T:(f + 1) * FF_T, :], preferred_element_type=F32)
        piece = acc[0:1, 0:1]
    yield piece
    store(x + mod_ref[5:6] * acc)


def _run(stages):
    for _ in stages:
        pass


def _zip_stages(vpu_stages, mxu_stages, plan):
    sides = {"v": vpu_stages, "m": mxu_stages}
    done = {"v": [next(vpu_stages)], "m": []}
    next(mxu_stages)
    for side, wait in plan:
        other = "m" if side == "v" else "v"
        try:
            done[side].append(sides[side].send(None if wait is None else done[other][wait]))
        except StopIteration:
            pass


L0_TILES = NTOK // CONV_T
L0_STEPS = L0_TILES // 2
L0_PLAN = (("v", None), ("v", None), ("m", 0), ("v", 0), ("m", None), ("v", 1), ("m", None),
           ("v", 2), ("v", 2), ("m", 4), ("v", 3), ("m", 6), ("v", 4), ("m", 8),
           ("v", None), ("m", None))


def _layer0_kernel(xp0_ref,
                   xpa_ref, xsa_ref, preva_ref, nexta_ref,
                   xpb_ref, xsb_ref, prevb_ref, nextb_ref,
                   mod0_ref, moda_ref, modb_ref,
                   ng0_ref, w1_ref, b1_ref, dw_ref, dwb_ref, lng_ref, lnb_ref, w2_ref,
                   ng1_ref, w13_ref, w2f_ref, cqkv_ref, cwo_ref,
                   o_ref, oqkv_ref, owo_ref,
                   buf_a, buf_b, g_scr, y_scr, g_scr_b, y_scr_b):
    s = pl.program_id(0)
    _cast_blocks([cqkv_ref, cwo_ref], [oqkv_ref, owo_ref])
    conv_w = (ng0_ref, w1_ref, b1_ref, dw_ref, dwb_ref, lng_ref, lnb_ref, w2_ref)
    ffn_w = (ng1_ref, w13_ref, w2f_ref)

    def put(ref, rows=slice(None)):
        def store(value):
            ref[rows] = value
        return store

    def conv(xp_ref, xs_ref, prev_ref, next_ref, tile, mod_ref, dst, scratch):
        x_main = jnp.where(tile < CONV_PT, xp_ref[...], xs_ref[...])
        return _conv_stages(x_main, prev_ref[...], next_ref[...], tile, mod_ref, conv_w,
                            *scratch, put(dst))

    @pl.when(s == 0)
    def _():
        _run(_conv_stages(xp0_ref[...], xp0_ref[0:HALO], xp0_ref[0:HALO], 0, mod0_ref,
                          conv_w, g_scr, y_scr, put(buf_a)))

    def ffn(x_ref, mod_ref, rows):
        return _ffn_stages(x_ref, mod_ref, *ffn_w, put(o_ref, rows))

    _zip_stages(conv(xpa_ref, xsa_ref, preva_ref, nexta_ref, 2 * s + 1, moda_ref, buf_b,
                     (g_scr, y_scr)),
                ffn(buf_a, mod0_ref, slice(0, CONV_T)), L0_PLAN)
    _zip_stages(conv(xpb_ref, xsb_ref, prevb_ref, nextb_ref,
                     jnp.minimum(2 * s + 2, L0_TILES - 1), modb_ref, buf_a,
                     (g_scr_b, y_scr_b)),
                ffn(buf_b, moda_ref, slice(CONV_T, 2 * CONV_T)), L0_PLAN)


def _layer0(xp, xs, mods, ng0, w1, b1, dw, dwb, lng, lnb, w2, ng1, w13, w2f, to_cast):
    hb = CONV_T // HALO
    n_hb = NS_TOK // HALO
    cast_specs, cast_shapes = _cast_specs(to_cast, L0_STEPS)

    def tile_specs(tile_of):
        def latent(s):
            return jnp.clip(tile_of(s) - CONV_PT, 0, L0_TILES - CONV_PT - 1)
        return [
            pl.BlockSpec((CONV_T, D), lambda s: (jnp.minimum(tile_of(s), CONV_PT - 1), 0)),
            pl.BlockSpec((CONV_T, D), lambda s: (latent(s), 0)),
            pl.BlockSpec((HALO, D), lambda s: (jnp.maximum(latent(s) * hb - 1, 0), 0)),
            pl.BlockSpec((HALO, D), lambda s: (jnp.minimum(latent(s) * hb + hb, n_hb - 1), 0)),
        ]

    def mod_spec(tile_of):
        return pl.BlockSpec((None, None, N_MOD, D),
                            lambda s: (0, _cond_row(tile_of(s), CONV_T), 0, 0))

    def resident(shape):
        return pl.BlockSpec(shape, lambda s: (0, 0), pipeline_mode=pl.Buffered(1))

    tile_0 = lambda s: 2 * s
    tile_a = lambda s: 2 * s + 1
    tile_b = lambda s: jnp.minimum(2 * s + 2, L0_TILES - 1)
    outs = pl.pallas_call(
        _layer0_kernel,
        out_shape=[jax.ShapeDtypeStruct((NTOK, D), F32)] + cast_shapes,
        grid=(L0_STEPS,),
        in_specs=[_const_spec((CONV_T, D))] + tile_specs(tile_a) + tile_specs(tile_b)
        + [mod_spec(tile_0), mod_spec(tile_a), mod_spec(tile_b),
           _const_spec((1, D)), resident((D, 2 * D)), _const_spec((1, 2 * D)),
           _const_spec((CONV_W, D)), _const_spec((1, D)), _const_spec((1, D)),
           _const_spec((1, D)), resident((D, D)),
           _const_spec((1, D)), resident((D, 2 * D_FF)), resident((D_FF, D))]
        + cast_specs,
        out_specs=[pl.BlockSpec((2 * CONV_T, D), lambda s: (s, 0))] + cast_specs,
        scratch_shapes=[pltpu.VMEM((CONV_T, D), F32), pltpu.VMEM((CONV_T, D), F32),
                        pltpu.VMEM((CONV_E, D), F32), pltpu.VMEM((CONV_T, D), F32),
                        pltpu.VMEM((CONV_E, D), F32), pltpu.VMEM((CONV_T, D), F32)],
        compiler_params=_cparams(1),
        name="layer0",
    )(xp, xp, xs, xs, xs, xp, xs, xs, xs, mods, mods, mods,
      ng0, w1, b1, dw, dwb, lng, lnb, w2, ng1, w13, w2f, *to_cast)
    return outs[0], outs[1:]


ROW_PT = NP_TOK // ROW_T
ROW_TPS = DEC_SEQ // ROW_T


def _rope_tables():
    half = HD // 2
    inv = ROPE_THETA ** (-jnp.arange(0, half, 2, dtype=F32) / half)
    t = jnp.arange(DEC_SEQ)
    lane = jnp.arange(LANES)
    jj = lane % HD
    pos = jnp.where(jj[None, :] < half, (t // GRID_W)[:, None], (t % GRID_W)[:, None]).astype(F32)
    ang = pos * inv[jj % (half // 2)][None, :]
    first = (jj % half) < (half // 2)
    cos = jnp.cos(ang)
    sin = jnp.where(first[None, :], -jnp.sin(ang), jnp.sin(ang))
    cos = jnp.concatenate([cos.reshape(ROW_TPS, ROW_T, LANES), jnp.ones((1, ROW_T, LANES), F32)])
    sin = jnp.concatenate([sin.reshape(ROW_TPS, ROW_T, LANES), jnp.zeros((1, ROW_T, LANES), F32)])
    return cos, sin


def _rope(x, cos, sin):
    lane = lax.broadcasted_iota(jnp.int32, x.shape, 1)
    first = (lane % (HD // 2)) < (HD // 4)
    swapped = jnp.where(first, pltpu.roll(x, LANES - HD // 4, axis=1), pltpu.roll(x, HD // 4, axis=1))
    return x * cos + swapped * sin


def _qkv_kernel(x_ref, mod_ref, ng_ref, w_ref, cos_ref, sin_ref,
                q_ref, k_ref, v_ref, sk_ref, sv_ref):
    i = pl.program_id(0)
    h = _norm_mod(x_ref[...], ng_ref[...], mod_ref[0:1], mod_ref[1:2]).astype(BF16)
    y = jnp.dot(h, w_ref[...], preferred_element_type=F32)
    k = y[:, D:D + KV_COLS]
    v = y[:, D + KV_COLS:]

    @pl.when(i < ROW_PT)
    def _():
        sk_ref[...] = k
        sv_ref[...] = v

    cos, sin = cos_ref[...], sin_ref[...]
    scale = HD ** -0.5 * LOG2E
    for c in range(D // LANES):
        cs = slice(c * LANES, (c + 1) * LANES)
        q_ref[:, cs] = (_rope(y[:, cs], cos, sin) * scale).astype(BF16)
    for c in range(KV_COLS // LANES):
        cs = slice(c * LANES, (c + 1) * LANES)
        k_ref[:, cs] = _rope(k[:, cs], cos, sin).astype(BF16)
    v_ref[...] = v.astype(BF16)


def _qkv_layer(x, mods, ng, wqkv, cos, sin):
    def tab(i):
        return (jnp.where(i < ROW_PT, ROW_TPS, (i - ROW_PT) % ROW_TPS), 0, 0)

    def state(i):
        return (jnp.minimum(i, ROW_PT - 1), 0)

    return pl.pallas_call(
        _qkv_kernel,
        out_shape=(jax.ShapeDtypeStruct((NTOK, D), BF16),
                   jax.ShapeDtypeStruct((NTOK, KV_COLS), BF16),
                   jax.ShapeDtypeStruct((NTOK, KV_COLS), BF16),
                   jax.ShapeDtypeStruct((NP_TOK, KV_COLS), F32),
                   jax.ShapeDtypeStruct((NP_TOK, KV_COLS), F32)),
        grid=(NTOK // ROW_T,),
        in_specs=[pl.BlockSpec((ROW_T, D), lambda i: (i, 0)),
                  _mod_spec(1, ROW_T),
                  _const_spec((1, D)),
                  _const_spec((D, QKV_COLS)),
                  pl.BlockSpec((None, ROW_T, LANES), tab),
                  pl.BlockSpec((None, ROW_T, LANES), tab)],
        out_specs=(pl.BlockSpec((ROW_T, D), lambda i: (i, 0)),
                   pl.BlockSpec((ROW_T, KV_COLS), lambda i: (i, 0)),
                   pl.BlockSpec((ROW_T, KV_COLS), lambda i: (i, 0)),
                   pl.BlockSpec((ROW_T, KV_COLS), state),
                   pl.BlockSpec((ROW_T, KV_COLS), state)),
        compiler_params=_cparams(1),
        name="qkv_rope",
    )(x, mods, ng, wqkv, cos, sin)


def _attn_heads(q, parts, sink_ref, o_ref):
    nt = (((1,), (1,)), ((), ()))
    tq = q.shape[0]
    for h in range(N_HEADS):
        kv = h // GQA
        hs = slice(kv * HD, (kv + 1) * HD)
        qh = q[:, h * HD:(h + 1) * HD]
        sink = sink_ref[h] * LOG2E
        scores = []
        m = jnp.full((tq, 1), sink, F32)
        for k, _, bias in parts:
            s = lax.dot_general(qh, k[:, hs], nt, preferred_element_type=F32)
            if bias is not None:
                s = s + bias
            scores.append(s)
            m = jnp.maximum(m, jnp.max(s, axis=-1, keepdims=True))
        denom = jnp.exp2(sink - m)
        o = jnp.zeros((tq, HD), F32)
        for s, (_, v, _) in zip(scores, parts):
            p = jnp.exp2(s - m)
            denom = denom + jnp.sum(p, axis=-1, keepdims=True)
            o = o + jnp.dot(p.astype(BF16), v[:, hs], preferred_element_type=F32)
        o_ref[:, h * HD:(h + 1) * HD] = (o / denom).astype(o_ref.dtype)


def _ctx_attn_kernel(sink_ref, q_ref, k_ref, v_ref, cast_ref, o_ref, cast_out_ref):
    _cast_blocks([cast_ref], [cast_out_ref])
    _attn_heads(q_ref[...], [(k_ref[...], v_ref[...], None)], sink_ref, o_ref)


def _lat_attn_kernel(sink_ref, q_ref, k_ref, v_ref, ck_ref, cv_ref, cast_ref,
                     o_ref, cast_out_ref):
    _cast_blocks([cast_ref], [cast_out_ref])
    j = pl.program_id(1)
    w0 = pl.multiple_of(jnp.clip(j * ATT_T - WINDOW, 0, DEC_SEQ - ATT_WIN), WINDOW)
    qpos = j * ATT_T + lax.broadcasted_iota(jnp.int32, (ATT_T, ATT_WIN), 0)
    kpos = w0 + lax.broadcasted_iota(jnp.int32, (ATT_T, ATT_WIN), 1)
    bias = jnp.where(jnp.abs(qpos - kpos) <= WINDOW, 0.0, NEG_INF)
    parts = [(k_ref[pl.ds(w0, ATT_WIN), :], v_ref[pl.ds(w0, ATT_WIN), :], bias),
             (ck_ref[...].astype(BF16), cv_ref[...].astype(BF16), None)]
    _attn_heads(q_ref[...], parts, sink_ref, o_ref)


def _attention(q, k, v, ck, cv, sink, cast_ctx, cast_lat):
    smem = pl.BlockSpec(memory_space=pltpu.SMEM)
    cast_specs, cast_shapes = _cast_specs([cast_ctx], BATCH)
    o_ctx, ctx_b = pl.pallas_call(
        _ctx_attn_kernel,
        out_shape=[jax.ShapeDtypeStruct((NP_TOK, D), BF16)] + cast_shapes,
        grid=(BATCH,),
        in_specs=[smem,
                  pl.BlockSpec((SEQ, D), lambda b: (b, 0)),
                  pl.BlockSpec((SEQ, KV_COLS), lambda b: (b, 0)),
                  pl.BlockSpec((SEQ, KV_COLS), lambda b: (b, 0))] + cast_specs,
        out_specs=[pl.BlockSpec((SEQ, D), lambda b: (b, 0))] + cast_specs,
        compiler_params=_cparams(1),
        name="ctx_attention",
    )(sink, q, k, v, cast_ctx)

    q_off = NP_TOK // ATT_T
    kv_off = NP_TOK // DEC_SEQ
    tps = DEC_SEQ // ATT_T
    cast_specs, cast_shapes = _cast_specs([cast_lat], DEC_BATCH * tps, lambda b, j: b * tps + j)
    o_lat, lat_b = pl.pallas_call(
        _lat_attn_kernel,
        out_shape=[jax.ShapeDtypeStruct((NS_TOK, D), BF16)] + cast_shapes,
        grid=(DEC_BATCH, tps),
        in_specs=[smem,
                  pl.BlockSpec((ATT_T, D), lambda b, j: (q_off + b * tps + j, 0)),
                  pl.BlockSpec((DEC_SEQ, KV_COLS), lambda b, j: (kv_off + b, 0)),
                  pl.BlockSpec((DEC_SEQ, KV_COLS), lambda b, j: (kv_off + b, 0)),
                  pl.BlockSpec((None, PAST, KV_COLS), lambda b, j: (b, 0, 0)),
                  pl.BlockSpec((None, PAST, KV_COLS), lambda b, j: (b, 0, 0))] + cast_specs,
        out_specs=[pl.BlockSpec((ATT_T, D), lambda b, j: (b * tps + j, 0))] + cast_specs,
        compiler_params=_cparams(2),
        name="latent_attention",
    )(sink, q, k, v, ck, cv, cast_lat)
    return o_ctx, o_lat, ctx_b, lat_b


def _wo_router_kernel(oc_ref, ol_ref, x_ref, mod_ref, ng_ref, wo_ref, r_ref,
                      x3_ref, h_ref, meta_ref, cnt_ref, carry):
    i = pl.program_id(0)

    @pl.when(i == 0)
    def _():
        carry[...] = jnp.zeros_like(carry)

    a = jnp.where(i < ROW_PT, oc_ref[...], ol_ref[...])
    o = jnp.dot(a, wo_ref[...], preferred_element_type=F32)
    x3 = x_ref[...] + mod_ref[2:3] * o
    x3_ref[...] = x3
    h = _norm_mod(x3, ng_ref[...], mod_ref[3:4], mod_ref[4:5])
    h_ref[...] = h

    logit = [jnp.sum(h * r_ref[e:e + 1, :], axis=-1, keepdims=True) for e in range(N_EXP)]

    def top1(cols):
        m = functools.reduce(jnp.maximum, cols)
        idx = jnp.full_like(m, float(N_EXP))
        for e in reversed(range(N_EXP)):
            idx = jnp.where(cols[e] == m, float(e), idx)
        return m, idx

    m1, e1 = top1(logit)
    m2, e2 = top1([jnp.where(e1 == float(e), -jnp.inf, logit[e]) for e in range(N_EXP)])
    lane = lax.broadcasted_iota(jnp.int32, (ROW_T, LANES), 1).astype(F32)
    t = jnp.exp(m2 - m1)
    g1 = 1.0 / (1.0 + t)
    g2 = t / (1.0 + t)

    oh1 = (lane == e1).astype(F32)
    oh2 = (lane == e2).astype(F32)
    oh = oh1 + oh2
    r_i = lax.broadcasted_iota(jnp.int32, (ROW_T, ROW_T), 0)
    c_i = lax.broadcasted_iota(jnp.int32, (ROW_T, ROW_T), 1)
    tri = (c_i < r_i).astype(BF16)
    before = jnp.dot(tri, oh.astype(BF16), preferred_element_type=F32) + carry[0:1, :]
    rank1 = jnp.sum(before * oh1, axis=-1, keepdims=True)
    rank2 = jnp.sum(before * oh2, axis=-1, keepdims=True)
    carry[0:1, :] = carry[0:1, :] + jnp.sum(oh, axis=0, keepdims=True)
    cnt_ref[...] = carry[...]

    meta = jnp.where(lane == 0, e1, 0.0)
    meta = jnp.where(lane == 1, e2, meta)
    meta = jnp.where(lane == 2, rank1, meta)
    meta = jnp.where(lane == 3, rank2, meta)
    meta = jnp.where(lane == 4, g1, meta)
    meta = jnp.where(lane == 5, g2, meta)
    meta_ref[...] = meta


def _wo_router(o_ctx, o_lat, x, mods, ng, wo, router_t):
    return pl.pallas_call(
        _wo_router_kernel,
        out_shape=(jax.ShapeDtypeStruct((NTOK, D), F32),
                   jax.ShapeDtypeStruct((NTOK, D), F32),
                   jax.ShapeDtypeStruct((NTOK, LANES), F32),
                   jax.ShapeDtypeStruct((SUBLANES, LANES), F32)),
        grid=(NTOK // ROW_T,),
        in_specs=[pl.BlockSpec((ROW_T, D), lambda i: (jnp.minimum(i, ROW_PT - 1), 0)),
                  pl.BlockSpec((ROW_T, D), lambda i: (jnp.maximum(i - ROW_PT, 0), 0)),
                  pl.BlockSpec((ROW_T, D), lambda i: (i, 0)),
                  _mod_spec(1, ROW_T),
                  _const_spec((1, D)),
                  _const_spec((D, D)),
                  _const_spec((N_EXP, D))],
        out_specs=(pl.BlockSpec((ROW_T, D), lambda i: (i, 0)),
                   pl.BlockSpec((ROW_T, D), lambda i: (i, 0)),
                   pl.BlockSpec((ROW_T, LANES), lambda i: (i, 0)),
                   pl.BlockSpec((SUBLANES, LANES), lambda i: (0, 0))),
        scratch_shapes=[pltpu.VMEM((SUBLANES, LANES), F32)],
        compiler_params=_cparams(1),
        name="wo_router",
    )(o_ctx, o_lat, x, mods, ng, wo, router_t)


MOE_NF = D_FFE // MOE_FT
assert MOE_NF == 2
MOE_STEPS = MOE_TILES + 1
Y_IDLE = 2 * NTOK + N_EXP * MOE_T
Y_ROWS = Y_IDLE + MOE_T
MAP_ROWS = (MOE_TILES + 2) * MOE_T
assert NTOK & (NTOK - 1) == 0


def _expert_kernel(te_ref, tv_ref, map_ref, h_ref, wu_ref, wg_ref, w2_ref, y_ref,
                   xbuf, xb, obuf, gsem, ssem):
    j = pl.program_id(0)
    f = pl.program_id(1)
    cur = j % 2
    nxt = 1 - cur
    computes = tv_ref[j + 1] == 1
    drains = jnp.logical_and(tv_ref[j] == 1, jnp.logical_not(computes))

    def gather(tile, slot, r):
        src = map_ref[(tile + 1) * MOE_T + r] & (NTOK - 1)
        return pltpu.make_async_copy(h_ref.at[pl.ds(src, 1)], xbuf.at[slot, pl.ds(r, 1)],
                                     gsem.at[slot])

    def gather_all(slot):
        return pltpu.make_async_copy(h_ref.at[pl.ds(0, MOE_T)], xbuf.at[slot], gsem.at[slot])

    def scatter(tile, slot, r):
        dst = map_ref[(tile + 1) * MOE_T + r]
        return pltpu.make_async_copy(obuf.at[slot, pl.ds(r, 1)], y_ref.at[pl.ds(dst, 1)],
                                     ssem.at[slot])

    def scatter_all(slot):
        return pltpu.make_async_copy(obuf.at[slot], y_ref.at[pl.ds(0, MOE_T)], ssem.at[slot])

    @pl.when(jnp.logical_and(j == 0, f == 0))
    def _():
        obuf[...] = jnp.zeros_like(obuf)
        for s in range(N_EXP):
            spare = pltpu.make_async_copy(
                obuf.at[0], y_ref.at[pl.ds(2 * NTOK + s * MOE_T, MOE_T)], ssem.at[0])
            spare.start()
            spare.wait()

        def first(r, c):
            gather(0, 0, r).start()
            scatter(-1, 0, r).start()
            return c

        lax.fori_loop(0, MOE_T, first, 0)

    @pl.when(jnp.logical_and(computes, f == 0))
    def _():
        gather_all(cur).wait()
        xb[...] = xbuf[cur].astype(BF16)
        for r in range(MOE_T):
            gather(j + 1, nxt, r).start()
        x = xb[...]
        u = jnp.dot(x, wu_ref[...], preferred_element_type=F32)
        g = jnp.dot(x, wg_ref[...], preferred_element_type=F32)
        a = (_silu(u) * g).astype(BF16)
        y = jnp.dot(a, w2_ref[...], preferred_element_type=F32)
        scatter_all(cur).wait()
        obuf[cur] = y

    @pl.when(jnp.logical_and(computes, f == 1))
    def _():
        for r in range(MOE_T):
            scatter(j - 1, nxt, r).start()
        x = xb[...]
        u = jnp.dot(x, wu_ref[...], preferred_element_type=F32)
        g = jnp.dot(x, wg_ref[...], preferred_element_type=F32)
        a = (_silu(u) * g).astype(BF16)
        y = jnp.dot(a, w2_ref[...], preferred_element_type=F32)
        obuf[cur] = obuf[cur] + y

    @pl.when(jnp.logical_and(drains, f == 0))
    def _():
        gather_all(cur).wait()
        scatter_all(cur).wait()

    @pl.when(jnp.logical_and(drains, f == 1))
    def _():
        def last(r, c):
            scatter(j - 1, nxt, r).start()
            return c

        lax.fori_loop(0, MOE_T, last, 0)
        scatter_all(nxt).wait()


def _experts(tile_e, tile_v, rowmap, h, w13, w2):
    def fidx(j, f, tv):
        return jnp.where(tv[j + 1] == 1, f, MOE_NF - 1)

    return pl.pallas_call(
        _expert_kernel,
        out_shape=jax.ShapeDtypeStruct((Y_ROWS, D), F32),
        grid_spec=pltpu.PrefetchScalarGridSpec(
            num_scalar_prefetch=3,
            grid=(MOE_STEPS, MOE_NF),
            in_specs=[
                pl.BlockSpec(memory_space=pl.ANY),
                pl.BlockSpec((None, D, MOE_FT), lambda j, f, te, tv, *_: (te[j], 0, fidx(j, f, tv))),
                pl.BlockSpec((None, D, MOE_FT),
                             lambda j, f, te, tv, *_: (te[j], 0, MOE_NF + fidx(j, f, tv))),
                pl.BlockSpec((None, MOE_FT, D), lambda j, f, te, tv, *_: (te[j], fidx(j, f, tv), 0)),
            ],
            out_specs=pl.BlockSpec(memory_space=pl.ANY),
            scratch_shapes=[pltpu.VMEM((2, MOE_T, D), F32), pltpu.VMEM((MOE_T, D), BF16),
                            pltpu.VMEM((2, MOE_T, D), F32),
                            pltpu.SemaphoreType.DMA((2,)), pltpu.SemaphoreType.DMA((2,))]),
        compiler_params=_cparams(2),
        name="moe_experts",
    )(tile_e, tile_v, rowmap, h, w13, w13, w2)


def _rowmap_kernel(pos_ref, spare_ref, map_ref, sem):
    fill = pltpu.make_async_copy(spare_ref, map_ref, sem)
    fill.start()
    fill.wait()

    def token(t, c):
        map_ref[pos_ref[t]] = t
        map_ref[pos_ref[NTOK + t]] = NTOK + t
        return c

    lax.fori_loop(0, NTOK, token, 0, unroll=16)


def _rowmap(pos, spare):
    smem = pl.BlockSpec(memory_space=pltpu.SMEM)
    return pl.pallas_call(
        _rowmap_kernel,
        out_shape=jax.ShapeDtypeStruct((MAP_ROWS,), jnp.int32),
        in_specs=[smem, pl.BlockSpec(memory_space=pl.ANY)],
        out_specs=smem,
        scratch_shapes=[pltpu.SemaphoreType.DMA],
        name="moe_rowmap",
    )(pos, spare)


def _combine_kernel(y1_ref, y2_ref, meta_ref, x_ref, mod_ref, fg_ref, yp_ref, ys_ref):
    i = pl.program_id(0)
    meta = meta_ref[...]
    y = meta[:, 4:5] * y1_ref[...] + meta[:, 5:6] * y2_ref[...]
    x = x_ref[...] + mod_ref[5:6] * y
    out = x * lax.rsqrt(jnp.mean(x * x, axis=-1, keepdims=True) + EPS) * fg_ref[...]

    @pl.when(i < ROW_PT)
    def _():
        yp_ref[...] = out

    @pl.when(i >= ROW_PT)
    def _():
        ys_ref[...] = out


def _combine(y, meta, x, mods, final_g):
    slot2 = NTOK // ROW_T
    return pl.pallas_call(
        _combine_kernel,
        out_shape=(jax.ShapeDtypeStruct((NP_TOK, D), F32),
                   jax.ShapeDtypeStruct((NS_TOK, D), F32)),
        grid=(NTOK // ROW_T,),
        in_specs=[pl.BlockSpec((ROW_T, D), lambda i: (i, 0)),
                  pl.BlockSpec((ROW_T, D), lambda i: (slot2 + i, 0)),
                  pl.BlockSpec((ROW_T, LANES), lambda i: (i, 0)),
                  pl.BlockSpec((ROW_T, D), lambda i: (i, 0)),
                  _mod_spec(1, ROW_T),
                  _const_spec((1, D))],
        out_specs=(pl.BlockSpec((ROW_T, D), lambda i: (jnp.minimum(i, ROW_PT - 1), 0)),
                   pl.BlockSpec((ROW_T, D), lambda i: (jnp.maximum(i - ROW_PT, 0), 0))),
        compiler_params=_cparams(1),
        name="moe_combine",
    )(y, y, meta, x, mods, final_g)


def _routing_tables(meta, counts):
    cnt = counts[0, :N_EXP].astype(jnp.int32)
    padded = (cnt + MOE_T - 1) // MOE_T * MOE_T
    ends = [padded[0]]
    for e in range(1, N_EXP):
        ends.append(ends[-1] + padded[e])
    ends = jnp.stack(ends)
    off = ends - padded

    def expert_of_row(r):
        return sum((ends[e] <= r).astype(jnp.int32) for e in range(N_EXP - 1))

    def lookup(table, e):
        return sum(jnp.where(e == i, table[i], 0) for i in range(N_EXP))

    e1, e2, r1, r2 = (meta[:, i].astype(jnp.int32) for i in range(4))
    pos = MOE_T + jnp.concatenate([lookup(off, e1) + r1, lookup(off, e2) + r2])

    n_valid = ends[-1] // MOE_T
    tile = jnp.arange(MOE_STEPS, dtype=jnp.int32)
    tile_e = expert_of_row(jnp.minimum(tile, n_valid - 1) * MOE_T)
    tile_v = jnp.concatenate([jnp.zeros((1,), jnp.int32), (tile < n_valid).astype(jnp.int32)])

    row = jnp.arange(MAP_ROWS, dtype=jnp.int32) - MOE_T
    row_e = expert_of_row(row)
    pad_rank = jnp.clip(row - lookup(off + cnt, row_e), 0, MOE_T - 1)
    idle = Y_IDLE + (row & (MOE_T - 1))
    spare = jnp.where((row < 0) | (row >= ends[-1]), idle, 2 * NTOK + row_e * MOE_T + pad_rank)
    return tile_e, tile_v, pos, spare


def kernel(x_prompt, x_sample, cache_k_l1, cache_v_l1, c, c_ctx, ada_w, ada_b, norm_g, conv_w1, conv_b1, conv_dw, conv_dwb, conv_ln_g, conv_ln_b, conv_w2, attn_wqkv, attn_wo, attn_sink, ffn_w13, ffn_w2, moe_router, moe_w13, moe_w2, final_g):
    xp = x_prompt.reshape(NP_TOK, D)
    xs = x_sample.reshape(NS_TOK, D)
    cond = jnp.concatenate([c, c_ctx[None, :], jnp.zeros((N_COND - DEC_BATCH - 1, D), F32)])
    mods = _ada(cond, ada_w, ada_b)

    row = lambda a: a.reshape(1, -1)
    x2, (wqkv_b, wo_b) = _layer0(
        xp, xs, mods, row(norm_g[0, 0]), conv_w1.astype(BF16), row(conv_b1),
        conv_dw, row(conv_dwb), row(conv_ln_g), row(conv_ln_b), conv_w2.astype(BF16),
        row(norm_g[0, 1]), ffn_w13.astype(BF16), ffn_w2.astype(BF16), [attn_wqkv, attn_wo])

    cos, sin = _rope_tables()
    q, k, v, state_k, state_v = _qkv_layer(x2, mods, row(norm_g[1, 0]), wqkv_b, cos, sin)
    o_ctx, o_lat, moe_w2_b, moe_w13_b = _attention(
        q, k, v, cache_k_l1.reshape(DEC_BATCH, PAST, KV_COLS),
        cache_v_l1.reshape(DEC_BATCH, PAST, KV_COLS), attn_sink,
        moe_w2.reshape(N_EXP * D_FFE, D), moe_w13.reshape(N_EXP * D, 2 * D_FFE))
    moe_w13_b = moe_w13_b.reshape(N_EXP, D, 2 * D_FFE)
    moe_w2_b = moe_w2_b.reshape(N_EXP, D_FFE, D)

    x3, h, meta, counts = _wo_router(o_ctx, o_lat, x2, mods, row(norm_g[1, 1]), wo_b,
                                     moe_router.T)
    tile_e, tile_v, pos, spare = _routing_tables(meta, counts)
    rowmap = _rowmap(pos, spare)
    y = _experts(tile_e, tile_v, rowmap, h, moe_w13_b, moe_w2_b)
    y_p, y_s = _combine(y, meta, x3, mods, row(final_g))

    return (y_p.reshape(BATCH, SEQ, D), y_s.reshape(DEC_BATCH, DEC_SEQ, D),
            state_k.reshape(BATCH, SEQ, N_KV, HD), state_v.reshape(BATCH, SEQ, N_KV, HD))
```

```python
import functools

import jax
import jax.numpy as jnp
from jax import lax
from jax.experimental import pallas as pl
from jax.experimental.pallas import tpu as pltpu

F32 = jnp.float32
BF16 = jnp.bfloat16

D = 1024
BATCH, SEQ = 32, 256
DEC_BATCH, DEC_SEQ = 4, 2048
PAST = 256
NP_TOK = BATCH * SEQ
NS_TOK = DEC_BATCH * DEC_SEQ
NTOK = NP_TOK + NS_TOK
GRID_W = 64
N_HEADS, N_KV, HD = 16, 4, 64
GQA = N_HEADS // N_KV
KV_COLS = N_KV * HD
QKV_COLS = D + 2 * KV_COLS
WINDOW = 128
ROPE_THETA = 10000.0
CONV_W = 31
CONV_PAD = CONV_W // 2
D_FF = 2816
N_EXP = 8
D_FFE = 3584
N_MOD = 6
EPS = 1e-6
NEG_INF = -1e30
LOG2E = 1.4426950408889634

N_COND = 8
CTX_ROW = DEC_BATCH

CONV_T = 256
HALO = 16
ROW_T = 512
ATT_T = 256
ATT_WIN = ATT_T + 2 * WINDOW
FF_T = 1408
MOE_T = 512
MOE_FT = 1792
MOE_TILES = 2 * NTOK // MOE_T + N_EXP
MOE_ROWS = MOE_TILES * MOE_T
LANES = 128
SUBLANES = 8

VMEM_LIMIT = 56 * 1024 * 1024


def _cparams(n_axes=1, vmem=VMEM_LIMIT, flags=None):
    return pltpu.CompilerParams(dimension_semantics=("arbitrary",) * n_axes,
                                vmem_limit_bytes=vmem, flags=flags)


def _silu(x):
    return x * jax.nn.sigmoid(x)


def _norm_mod(x, g, shift, scale):
    y = x * lax.rsqrt(jnp.mean(x * x, axis=-1, keepdims=True) + EPS)
    return (y * g) * (1.0 + scale) + shift


def _cond_row(tile, tile_rows):
    n_prompt_tiles = NP_TOK // tile_rows
    per_seq = DEC_SEQ // tile_rows
    return jnp.where(tile < n_prompt_tiles, CTX_ROW, (tile - n_prompt_tiles) // per_seq)


def _mod_spec(layer, tile_rows):
    return pl.BlockSpec((None, None, N_MOD, D),
                        lambda i, *_: (layer, _cond_row(i, tile_rows), 0, 0))


def _const_spec(shape):
    nd = len(shape)
    return pl.BlockSpec(shape, lambda *_: (0,) * nd)


BF16_SUBLANES = 16


def _cast_specs(arrays, steps, step_of=lambda i: i):
    specs, shapes = [], []
    for a in arrays:
        rows, cols = a.shape
        n = steps
        while rows % n or (rows // n) % BF16_SUBLANES:
            n -= 1
        specs.append(pl.BlockSpec((rows // n, cols),
                                  lambda *g, n=n: (jnp.minimum(step_of(*g), n - 1), 0)))
        shapes.append(jax.ShapeDtypeStruct((rows, cols), BF16))
    return specs, shapes


def _cast_blocks(in_refs, out_refs):
    for w_ref, o_ref in zip(in_refs, out_refs):
        o_ref[...] = w_ref[...].astype(BF16)


ADA_TN = 1536


def _ada_kernel(cond_ref, w_ref, b_ref, o_ref):
    s = _silu(cond_ref[...]).astype(BF16)
    o_ref[...] = jnp.dot(s, w_ref[...].astype(BF16), preferred_element_type=F32) + b_ref[...]


def _ada(cond, ada_w, ada_b):
    depth = ada_w.shape[0]
    out = pl.pallas_call(
        _ada_kernel,
        out_shape=jax.ShapeDtypeStruct((depth, N_COND, N_MOD * D), F32),
        grid=(depth, N_MOD * D // ADA_TN),
        in_specs=[pl.BlockSpec((N_COND, D), lambda l, n: (0, 0)),
                  pl.BlockSpec((None, D, ADA_TN), lambda l, n: (l, 0, n)),
                  pl.BlockSpec((None, 1, ADA_TN), lambda l, n: (l, 0, n))],
        out_specs=pl.BlockSpec((None, N_COND, ADA_TN), lambda l, n: (l, 0, n)),
        compiler_params=_cparams(2),
        name="adaln",
    )(cond, ada_w, ada_b.reshape(depth, 1, N_MOD * D))
    return out.reshape(depth, N_COND, N_MOD, D)


CONV_PT = NP_TOK // CONV_T
CONV_TPS = DEC_SEQ // CONV_T
CONV_E = CONV_T + 2 * HALO
CONV_RB = 128


def _hold(piece):
    return jnp.where(piece != piece, piece, 0.0)


def _conv_stages(x_main, x_prev, x_next, tile, mod_ref, weights, g_scr, y_scr, store):
    ng_ref, w1_ref, b1_ref, dw_ref, dwb_ref, lng_ref, lnb_ref, w2_ref = weights
    j = tile - CONV_PT
    has_prev = jnp.logical_and(tile >= CONV_PT, j % CONV_TPS != 0)
    has_next = jnp.logical_and(tile >= CONV_PT, j % CONV_TPS != CONV_TPS - 1)
    shift, scale, gate = mod_ref[0:1], mod_ref[1:2], mod_ref[2:3]

    xe = jnp.concatenate([x_prev, x_main, x_next], axis=0)
    h = _norm_mod(xe, ng_ref[...], shift, scale).astype(BF16)
    uv = jnp.dot(h, w1_ref[...], preferred_element_type=F32) + b1_ref[...]
    g = uv[:, :D] * jax.nn.sigmoid(uv[:, D:])
    g_scr[0:HALO] = jnp.where(has_prev, g[0:HALO], 0.0)
    g_scr[HALO:HALO + CONV_T] = g[HALO:HALO + CONV_T]
    g_scr[HALO + CONV_T:] = jnp.where(has_next, g[HALO + CONV_T:], 0.0)
    tie = yield g[0:1, 0:1]

    base = HALO - CONV_PAD
    sub = SUBLANES
    for c in range(D // LANES):
        cs = slice(c * LANES, (c + 1) * LANES)
        for t0 in range(0, CONV_T, CONV_RB):
            y = dwb_ref[:, cs] if tie is None else dwb_ref[:, cs] + _hold(tie)
            for r in range(sub):
                z = None
                for a in range((base + CONV_W - 1) // sub + 1):
                    k = sub * a + r - base
                    if 0 <= k < CONV_W:
                        rows = slice(t0 + sub * a, t0 + sub * a + CONV_RB + sub)
                        term = g_scr[rows, cs] * dw_ref[k:k + 1, cs]
                        z = term if z is None else z + term
                y = y + z[r:r + CONV_RB]
            y_scr[t0:t0 + CONV_RB, cs] = y
        tie = yield y[0:1, 0:1]

    y = y_scr[...]
    mu = jnp.mean(y, axis=-1, keepdims=True)
    d = y - mu
    var = jnp.mean(d * d, axis=-1, keepdims=True)
    yn = d * lax.rsqrt(var + EPS) * lng_ref[...] + lnb_ref[...]
    o = jnp.dot(_silu(yn).astype(BF16), w2_ref[...], preferred_element_type=F32)
    store(x_main + gate * o)


def _ffn_stages(x_ref, mod_ref, ng_ref, w13_ref, w2_ref, store):
    x = x_ref[...]
    h = _norm_mod(x, ng_ref[...], mod_ref[3:4], mod_ref[4:5]).astype(BF16)

    def tied(v, tie):
        return v if tie is None else v + _hold(tie).astype(v.dtype)

    acc = jnp.zeros(x.shape, F32)
    piece = None
    for f in range(D_FF // FF_T):
        tie = yield piece
        u = jnp.dot(tied(h, tie), w13_ref[:, f * FF_T:(f + 1) * FF_T],
                    preferred_element_type=F32)
        tie = yield u[0:1, 0:1]
        g = jnp.dot(tied(h, tie), w13_ref[:, D_FF + f * FF_T:D_FF + (f + 1) * FF_T],
                    preferred_element_type=F32)
        tie = yield g[0:1, 0:1]
        a = tied(_silu(u) * g, tie).astype(BF16)
        acc = acc + jnp.dot(a, w2_ref[f * FF_T:(f + 1) * FF_T, :], preferred_element_type=F32)
        piece = acc[0:1, 0:1]
    yield piece
    store(x + mod_ref[5:6] * acc)


def _run(stages):
    for _ in stages:
        pass


def _zip_stages(vpu_stages, mxu_stages, plan):
    sides = {"v": vpu_stages, "m": mxu_stages}
    done = {"v": [next(vpu_stages)], "m": []}
    next(mxu_stages)
    for side, wait in plan:
        other = "m" if side == "v" else "v"
        try:
            done[side].append(sides[side].send(None if wait is None else done[other][wait]))
        except StopIteration:
            pass


L0_TILES = NTOK // CONV_T
L0_STEPS = L0_TILES // 2
L0_PLAN = (("v", None), ("v", None), ("m", 0), ("v", 0), ("m", None), ("v", 1), ("m", None),
           ("v", 2), ("v", 2), ("m", 4), ("v", 3), ("m", 6), ("v", 4), ("m", 8),
           ("v", None), ("m", None))


def _layer0_kernel(xp0_ref,
                   xpa_ref, xsa_ref, preva_ref, nexta_ref,
                   xpb_ref, xsb_ref, prevb_ref, nextb_ref,
                   mod0_ref, moda_ref, modb_ref,
                   ng0_ref, w1_ref, b1_ref, dw_ref, dwb_ref, lng_ref, lnb_ref, w2_ref,
                   ng1_ref, w13_ref, w2f_ref, cqkv_ref, cwo_ref,
                   o_ref, oqkv_ref, owo_ref,
                   buf_a, buf_b, g_scr, y_scr, g_scr_b, y_scr_b):
    s = pl.program_id(0)
    _cast_blocks([cqkv_ref, cwo_ref], [oqkv_ref, owo_ref])
    conv_w = (ng0_ref, w1_ref, b1_ref, dw_ref, dwb_ref, lng_ref, lnb_ref, w2_ref)
    ffn_w = (ng1_ref, w13_ref, w2f_ref)

    def put(ref, rows=slice(None)):
        def store(value):
            ref[rows] = value
        return store

    def conv(xp_ref, xs_ref, prev_ref, next_ref, tile, mod_ref, dst, scratch):
        x_main = jnp.where(tile < CONV_PT, xp_ref[...], xs_ref[...])
        return _conv_stages(x_main, prev_ref[...], next_ref[...], tile, mod_ref, conv_w,
                            *scratch, put(dst))

    @pl.when(s == 0)
    def _():
        _run(_conv_stages(xp0_ref[...], xp0_ref[0:HALO], xp0_ref[0:HALO], 0, mod0_ref,
                          conv_w, g_scr, y_scr, put(buf_a)))

    def ffn(x_ref, mod_ref, rows):
        return _ffn_stages(x_ref, mod_ref, *ffn_w, put(o_ref, rows))

    _zip_stages(conv(xpa_ref, xsa_ref, preva_ref, nexta_ref, 2 * s + 1, moda_ref, buf_b,
                     (g_scr, y_scr)),
                ffn(buf_a, mod0_ref, slice(0, CONV_T)), L0_PLAN)
    _zip_stages(conv(xpb_ref, xsb_ref, prevb_ref, nextb_ref,
                     jnp.minimum(2 * s + 2, L0_TILES - 1), modb_ref, buf_a,
                     (g_scr_b, y_scr_b)),
                ffn(buf_b, moda_ref, slice(CONV_T, 2 * CONV_T)), L0_PLAN)


def _layer0(xp, xs, mods, ng0, w1, b1, dw, dwb, lng, lnb, w2, ng1, w13, w2f, to_cast):
    hb = CONV_T // HALO
    n_hb = NS_TOK // HALO
    cast_specs, cast_shapes = _cast_specs(to_cast, L0_STEPS)

    def tile_specs(tile_of):
        def latent(s):
            return jnp.clip(tile_of(s) - CONV_PT, 0, L0_TILES - CONV_PT - 1)
        return [
            pl.BlockSpec((CONV_T, D), lambda s: (jnp.minimum(tile_of(s), CONV_PT - 1), 0)),
            pl.BlockSpec((CONV_T, D), lambda s: (latent(s), 0)),
            pl.BlockSpec((HALO, D), lambda s: (jnp.maximum(latent(s) * hb - 1, 0), 0)),
            pl.BlockSpec((HALO, D), lambda s: (jnp.minimum(latent(s) * hb + hb, n_hb - 1), 0)),
        ]

    def mod_spec(tile_of):
        return pl.BlockSpec((None, None, N_MOD, D),
                            lambda s: (0, _cond_row(tile_of(s), CONV_T), 0, 0))

    def resident(shape):
        return pl.BlockSpec(shape, lambda s: (0, 0), pipeline_mode=pl.Buffered(1))

    tile_0 = lambda s: 2 * s
    tile_a = lambda s: 2 * s + 1
    tile_b = lambda s: jnp.minimum(2 * s + 2, L0_TILES - 1)
    outs = pl.pallas_call(
        _layer0_kernel,
        out_shape=[jax.ShapeDtypeStruct((NTOK, D), F32)] + cast_shapes,
        grid=(L0_STEPS,),
        in_specs=[_const_spec((CONV_T, D))] + tile_specs(tile_a) + tile_specs(tile_b)
        + [mod_spec(tile_0), mod_spec(tile_a), mod_spec(tile_b),
           _const_spec((1, D)), resident((D, 2 * D)), _const_spec((1, 2 * D)),
           _const_spec((CONV_W, D)), _const_spec((1, D)), _const_spec((1, D)),
           _const_spec((1, D)), resident((D, D)),
           _const_spec((1, D)), resident((D, 2 * D_FF)), resident((D_FF, D))]
        + cast_specs,
        out_specs=[pl.BlockSpec((2 * CONV_T, D), lambda s: (s, 0))] + cast_specs,
        scratch_shapes=[pltpu.VMEM((CONV_T, D), F32), pltpu.VMEM((CONV_T, D), F32),
                        pltpu.VMEM((CONV_E, D), F32), pltpu.VMEM((CONV_T, D), F32),
                        pltpu.VMEM((CONV_E, D), F32), pltpu.VMEM((CONV_T, D), F32)],
        compiler_params=_cparams(1),
        name="layer0",
    )(xp, xp, xs, xs, xs, xp, xs, xs, xs, mods, mods, mods,
      ng0, w1, b1, dw, dwb, lng, lnb, w2, ng1, w13, w2f, *to_cast)
    return outs[0], outs[1:]


ROW_PT = NP_TOK // ROW_T
ROW_TPS = DEC_SEQ // ROW_T


def _rope_tables():
    half = HD // 2
    inv = ROPE_THETA ** (-jnp.arange(0, half, 2, dtype=F32) / half)
    t = jnp.arange(DEC_SEQ)
    lane = jnp.arange(LANES)
    jj = lane % HD
    pos = jnp.where(jj[None, :] < half, (t // GRID_W)[:, None], (t % GRID_W)[:, None]).astype(F32)
    ang = pos * inv[jj % (half // 2)][None, :]
    first = (jj % half) < (half // 2)
    cos = jnp.cos(ang)
    sin = jnp.where(first[None, :], -jnp.sin(ang), jnp.sin(ang))
    cos = jnp.concatenate([cos.reshape(ROW_TPS, ROW_T, LANES), jnp.ones((1, ROW_T, LANES), F32)])
    sin = jnp.concatenate([sin.reshape(ROW_TPS, ROW_T, LANES), jnp.zeros((1, ROW_T, LANES), F32)])
    return cos, sin


def _rope(x, cos, sin):
    lane = lax.broadcasted_iota(jnp.int32, x.shape, 1)
    first = (lane % (HD // 2)) < (HD // 4)
    swapped = jnp.where(first, pltpu.roll(x, LANES - HD // 4, axis=1), pltpu.roll(x, HD // 4, axis=1))
    return x * cos + swapped * sin


def _qkv_kernel(x_ref, mod_ref, ng_ref, w_ref, cos_ref, sin_ref,
                q_ref, k_ref, v_ref, sk_ref, sv_ref):
    i = pl.program_id(0)
    h = _norm_mod(x_ref[...], ng_ref[...], mod_ref[0:1], mod_ref[1:2]).astype(BF16)
    y = jnp.dot(h, w_ref[...], preferred_element_type=F32)
    k = y[:, D:D + KV_COLS]
    v = y[:, D + KV_COLS:]

    scale = HD ** -0.5 * LOG2E

    @pl.when(i < ROW_PT)
    def _():
        sk_ref[...] = k
        sv_ref[...] = v
        q_ref[...] = (y[:, :D] * scale).astype(BF16)
        k_ref[...] = k.astype(BF16)

    @pl.when(i >= ROW_PT)
    def _():
        cos, sin = cos_ref[...], sin_ref[...]
        for c in range(D // LANES):
            cs = slice(c * LANES, (c + 1) * LANES)
            q_ref[:, cs] = (_rope(y[:, cs], cos, sin) * scale).astype(BF16)
        for c in range(KV_COLS // LANES):
            cs = slice(c * LANES, (c + 1) * LANES)
            k_ref[:, cs] = _rope(k[:, cs], cos, sin).astype(BF16)

    v_ref[...] = v.astype(BF16)


def _qkv_layer(x, mods, ng, wqkv, cos, sin):
    def tab(i):
        return (jnp.where(i < ROW_PT, ROW_TPS, (i - ROW_PT) % ROW_TPS), 0, 0)

    def state(i):
        return (jnp.minimum(i, ROW_PT - 1), 0)

    return pl.pallas_call(
        _qkv_kernel,
        out_shape=(jax.ShapeDtypeStruct((NTOK, D), BF16),
                   jax.ShapeDtypeStruct((NTOK, KV_COLS), BF16),
                   jax.ShapeDtypeStruct((NTOK, KV_COLS), BF16),
                   jax.ShapeDtypeStruct((NP_TOK, KV_COLS), F32),
                   jax.ShapeDtypeStruct((NP_TOK, KV_COLS), F32)),
        grid=(NTOK // ROW_T,),
        in_specs=[pl.BlockSpec((ROW_T, D), lambda i: (i, 0)),
                  _mod_spec(1, ROW_T),
                  _const_spec((1, D)),
                  _const_spec((D, QKV_COLS)),
                  pl.BlockSpec((None, ROW_T, LANES), tab),
                  pl.BlockSpec((None, ROW_T, LANES), tab)],
        out_specs=(pl.BlockSpec((ROW_T, D), lambda i: (i, 0)),
                   pl.BlockSpec((ROW_T, KV_COLS), lambda i: (i, 0)),
                   pl.BlockSpec((ROW_T, KV_COLS), lambda i: (i, 0)),
                   pl.BlockSpec((ROW_T, KV_COLS), state),
                   pl.BlockSpec((ROW_T, KV_COLS), state)),
        compiler_params=_cparams(1),
        name="qkv_rope",
    )(x, mods, ng, wqkv, cos, sin)


def _attn_heads(q, parts, sink_ref, o_ref):
    nt = (((1,), (1,)), ((), ()))
    tq = q.shape[0]
    for h in range(N_HEADS):
        kv = h // GQA
        hs = slice(kv * HD, (kv + 1) * HD)
        qh = q[:, h * HD:(h + 1) * HD]
        sink = sink_ref[h] * LOG2E
        scores = []
        m = jnp.full((tq, 1), sink, F32)
        for k, _, bias in parts:
            s = lax.dot_general(qh, k[:, hs], nt, preferred_element_type=F32)
            if bias is not None:
                s = s + bias
            scores.append(s)
            m = jnp.maximum(m, jnp.max(s, axis=-1, keepdims=True))
        denom = jnp.exp2(sink - m)
        o = jnp.zeros((tq, HD), F32)
        for s, (_, v, _) in zip(scores, parts):
            p = jnp.exp2(s - m)
            denom = denom + jnp.sum(p, axis=-1, keepdims=True)
            o = o + jnp.dot(p.astype(BF16), v[:, hs], preferred_element_type=F32)
        o_ref[:, h * HD:(h + 1) * HD] = (o / denom).astype(o_ref.dtype)


def _ctx_attn_kernel(sink_ref, q_ref, k_ref, v_ref, cast_ref, o_ref, cast_out_ref):
    _cast_blocks([cast_ref], [cast_out_ref])
    _attn_heads(q_ref[...], [(k_ref[...], v_ref[...], None)], sink_ref, o_ref)


def _lat_attn_kernel(sink_ref, q_ref, k_ref, v_ref, ck_ref, cv_ref, cast_ref,
                     o_ref, cast_out_ref):
    _cast_blocks([cast_ref], [cast_out_ref])
    j = pl.program_id(1)
    w0 = pl.multiple_of(jnp.clip(j * ATT_T - WINDOW, 0, DEC_SEQ - ATT_WIN), WINDOW)
    qpos = j * ATT_T + lax.broadcasted_iota(jnp.int32, (ATT_T, ATT_WIN), 0)
    kpos = w0 + lax.broadcasted_iota(jnp.int32, (ATT_T, ATT_WIN), 1)
    bias = jnp.where(jnp.abs(qpos - kpos) <= WINDOW, 0.0, NEG_INF)
    parts = [(k_ref[pl.ds(w0, ATT_WIN), :], v_ref[pl.ds(w0, ATT_WIN), :], bias),
             (ck_ref[...].astype(BF16), cv_ref[...].astype(BF16), None)]
    _attn_heads(q_ref[...], parts, sink_ref, o_ref)


def _attention(q, k, v, ck, cv, sink, cast_ctx, cast_lat):
    smem = pl.BlockSpec(memory_space=pltpu.SMEM)
    cast_specs, cast_shapes = _cast_specs([cast_ctx], BATCH)
    o_ctx, ctx_b = pl.pallas_call(
        _ctx_attn_kernel,
        out_shape=[jax.ShapeDtypeStruct((NP_TOK, D), BF16)] + cast_shapes,
        grid=(BATCH,),
        in_specs=[smem,
                  pl.BlockSpec((SEQ, D), lambda b: (b, 0)),
                  pl.BlockSpec((SEQ, KV_COLS), lambda b: (b, 0)),
                  pl.BlockSpec((SEQ, KV_COLS), lambda b: (b, 0))] + cast_specs,
        out_specs=[pl.BlockSpec((SEQ, D), lambda b: (b, 0))] + cast_specs,
        compiler_params=_cparams(1),
        name="ctx_attention",
    )(sink, q, k, v, cast_ctx)

    q_off = NP_TOK // ATT_T
    kv_off = NP_TOK // DEC_SEQ
    tps = DEC_SEQ // ATT_T
    cast_specs, cast_shapes = _cast_specs([cast_lat], DEC_BATCH * tps, lambda b, j: b * tps + j)
    o_lat, lat_b = pl.pallas_call(
        _lat_attn_kernel,
        out_shape=[jax.ShapeDtypeStruct((NS_TOK, D), BF16)] + cast_shapes,
        grid=(DEC_BATCH, tps),
        in_specs=[smem,
                  pl.BlockSpec((ATT_T, D), lambda b, j: (q_off + b * tps + j, 0)),
                  pl.BlockSpec((DEC_SEQ, KV_COLS), lambda b, j: (kv_off + b, 0)),
                  pl.BlockSpec((DEC_SEQ, KV_COLS), lambda b, j: (kv_off + b, 0)),
                  pl.BlockSpec((None, PAST, KV_COLS), lambda b, j: (b, 0, 0)),
                  pl.BlockSpec((None, PAST, KV_COLS), lambda b, j: (b, 0, 0))] + cast_specs,
        out_specs=[pl.BlockSpec((ATT_T, D), lambda b, j: (b * tps + j, 0))] + cast_specs,
        compiler_params=_cparams(2),
        name="latent_attention",
    )(sink, q, k, v, ck, cv, cast_lat)
    return o_ctx, o_lat, ctx_b, lat_b


def _wo_router_kernel(oc_ref, ol_ref, x_ref, mod_ref, ng_ref, wo_ref, r_ref,
                      x3_ref, h_ref, meta_ref, meta_t_ref, cnt_ref, carry):
    i = pl.program_id(0)

    @pl.when(i == 0)
    def _():
        carry[...] = jnp.zeros_like(carry)

    a = jnp.where(i < ROW_PT, oc_ref[...], ol_ref[...])
    o = jnp.dot(a, wo_ref[...], preferred_element_type=F32)
    x3 = x_ref[...] + mod_ref[2:3] * o
    x3_ref[...] = x3
    h = _norm_mod(x3, ng_ref[...], mod_ref[3:4], mod_ref[4:5])
    h_ref[...] = h

    logit = [jnp.sum(h * r_ref[e:e + 1, :], axis=-1, keepdims=True) for e in range(N_EXP)]

    def top1(cols):
        m = functools.reduce(jnp.maximum, cols)
        idx = jnp.full_like(m, float(N_EXP))
        for e in reversed(range(N_EXP)):
            idx = jnp.where(cols[e] == m, float(e), idx)
        return m, idx

    m1, e1 = top1(logit)
    m2, e2 = top1([jnp.where(e1 == float(e), -jnp.inf, logit[e]) for e in range(N_EXP)])
    lane = lax.broadcasted_iota(jnp.int32, (ROW_T, LANES), 1).astype(F32)
    t = jnp.exp(m2 - m1)
    g1 = 1.0 / (1.0 + t)
    g2 = t / (1.0 + t)

    oh1 = (lane == e1).astype(F32)
    oh2 = (lane == e2).astype(F32)
    oh = oh1 + oh2
    r_i = lax.broadcasted_iota(jnp.int32, (ROW_T, ROW_T), 0)
    c_i = lax.broadcasted_iota(jnp.int32, (ROW_T, ROW_T), 1)
    tri = (c_i < r_i).astype(BF16)
    before = jnp.dot(tri, oh.astype(BF16), preferred_element_type=F32) + carry[0:1, :]
    rank1 = jnp.sum(before * oh1, axis=-1, keepdims=True)
    rank2 = jnp.sum(before * oh2, axis=-1, keepdims=True)
    carry[0:1, :] = carry[0:1, :] + jnp.sum(oh, axis=0, keepdims=True)
    cnt_ref[...] = carry[...]

    meta = jnp.where(lane == 0, e1, 0.0)
    meta = jnp.where(lane == 1, e2, meta)
    meta = jnp.where(lane == 2, rank1, meta)
    meta = jnp.where(lane == 3, rank2, meta)
    meta = jnp.where(lane == 4, g1, meta)
    meta = jnp.where(lane == 5, g2, meta)
    meta_ref[...] = meta
    meta_t_ref[...] = meta.T[0:SUBLANES, :]


def _wo_router(o_ctx, o_lat, x, mods, ng, wo, router_t):
    return pl.pallas_call(
        _wo_router_kernel,
        out_shape=(jax.ShapeDtypeStruct((NTOK, D), F32),
                   jax.ShapeDtypeStruct((NTOK, D), F32),
                   jax.ShapeDtypeStruct((NTOK, LANES), F32),
                   jax.ShapeDtypeStruct((SUBLANES, NTOK), F32),
                   jax.ShapeDtypeStruct((SUBLANES, LANES), F32)),
        grid=(NTOK // ROW_T,),
        in_specs=[pl.BlockSpec((ROW_T, D), lambda i: (jnp.minimum(i, ROW_PT - 1), 0)),
                  pl.BlockSpec((ROW_T, D), lambda i: (jnp.maximum(i - ROW_PT, 0), 0)),
                  pl.BlockSpec((ROW_T, D), lambda i: (i, 0)),
                  _mod_spec(1, ROW_T),
                  _const_spec((1, D)),
                  _const_spec((D, D)),
                  _const_spec((N_EXP, D))],
        out_specs=(pl.BlockSpec((ROW_T, D), lambda i: (i, 0)),
                   pl.BlockSpec((ROW_T, D), lambda i: (i, 0)),
                   pl.BlockSpec((ROW_T, LANES), lambda i: (i, 0)),
                   pl.BlockSpec((SUBLANES, ROW_T), lambda i: (0, i)),
                   pl.BlockSpec((SUBLANES, LANES), lambda i: (0, 0))),
        scratch_shapes=[pltpu.VMEM((SUBLANES, LANES), F32)],
        compiler_params=_cparams(1),
        name="wo_router",
    )(o_ctx, o_lat, x, mods, ng, wo, router_t)


MOE_NF = D_FFE // MOE_FT
assert MOE_NF == 2
MOE_STEPS = MOE_TILES + 1
Y_IDLE = 2 * NTOK + N_EXP * MOE_T
Y_ROWS = Y_IDLE + MOE_T
MAP_ROWS = (MOE_TILES + 2) * MOE_T
assert NTOK & (NTOK - 1) == 0


def _expert_kernel(te_ref, tv_ref, map_ref, h_ref, wu_ref, wg_ref, w2_ref, y_ref,
                   xbuf, xb, obuf, gsem, ssem):
    j = pl.program_id(0)
    f = pl.program_id(1)
    cur = j % 2
    nxt = 1 - cur
    computes = tv_ref[j + 1] == 1
    drains = jnp.logical_and(tv_ref[j] == 1, jnp.logical_not(computes))

    def gather(tile, slot, r):
        src = map_ref[(tile + 1) * MOE_T + r] & (NTOK - 1)
        return pltpu.make_async_copy(h_ref.at[pl.ds(src, 1)], xbuf.at[slot, pl.ds(r, 1)],
                                     gsem.at[slot])

    def gather_all(slot):
        return pltpu.make_async_copy(h_ref.at[pl.ds(0, MOE_T)], xbuf.at[slot], gsem.at[slot])

    def scatter(tile, slot, r):
        dst = map_ref[(tile + 1) * MOE_T + r]
        return pltpu.make_async_copy(obuf.at[slot, pl.ds(r, 1)], y_ref.at[pl.ds(dst, 1)],
                                     ssem.at[slot])

    def scatter_all(slot):
        return pltpu.make_async_copy(obuf.at[slot], y_ref.at[pl.ds(0, MOE_T)], ssem.at[slot])

    @pl.when(jnp.logical_and(j == 0, f == 0))
    def _():
        obuf[...] = jnp.zeros_like(obuf)
        for s in range(N_EXP):
            spare = pltpu.make_async_copy(
                obuf.at[0], y_ref.at[pl.ds(2 * NTOK + s * MOE_T, MOE_T)], ssem.at[0])
            spare.start()
            spare.wait()

        def first(r, c):
            gather(0, 0, r).start()
            scatter(-1, 0, r).start()
            return c

        lax.fori_loop(0, MOE_T, first, 0)

    @pl.when(jnp.logical_and(computes, f == 0))
    def _():
        gather_all(cur).wait()
        xb[...] = xbuf[cur].astype(BF16)
        for r in range(MOE_T):
            gather(j + 1, nxt, r).start()
        x = xb[...]
        u = jnp.dot(x, wu_ref[...], preferred_element_type=F32)
        g = jnp.dot(x, wg_ref[...], preferred_element_type=F32)
        a = (_silu(u) * g).astype(BF16)
        y = jnp.dot(a, w2_ref[...], preferred_element_type=F32)
        scatter_all(cur).wait()
        obuf[cur] = y

    @pl.when(jnp.logical_and(computes, f == 1))
    def _():
        for r in range(MOE_T):
            scatter(j - 1, nxt, r).start()
        x = xb[...]
        u = jnp.dot(x, wu_ref[...], preferred_element_type=F32)
        g = jnp.dot(x, wg_ref[...], preferred_element_type=F32)
        a = (_silu(u) * g).astype(BF16)
        y = jnp.dot(a, w2_ref[...], preferred_element_type=F32)
        obuf[cur] = obuf[cur] + y

    @pl.when(jnp.logical_and(drains, f == 0))
    def _():
        gather_all(cur).wait()
        scatter_all(cur).wait()

    @pl.when(jnp.logical_and(drains, f == 1))
    def _():
        def last(r, c):
            scatter(j - 1, nxt, r).start()
            return c

        lax.fori_loop(0, MOE_T, last, 0)
        scatter_all(nxt).wait()


def _experts(tile_e, tile_v, rowmap, h, w13, w2):
    def fidx(j, f, tv):
        return jnp.where(tv[j + 1] == 1, f, MOE_NF - 1)

    return pl.pallas_call(
        _expert_kernel,
        out_shape=jax.ShapeDtypeStruct((Y_ROWS, D), F32),
        grid_spec=pltpu.PrefetchScalarGridSpec(
            num_scalar_prefetch=3,
            grid=(MOE_STEPS, MOE_NF),
            in_specs=[
                pl.BlockSpec(memory_space=pl.ANY),
                pl.BlockSpec((None, D, MOE_FT), lambda j, f, te, tv, *_: (te[j], 0, fidx(j, f, tv))),
                pl.BlockSpec((None, D, MOE_FT),
                             lambda j, f, te, tv, *_: (te[j], 0, MOE_NF + fidx(j, f, tv))),
                pl.BlockSpec((None, MOE_FT, D), lambda j, f, te, tv, *_: (te[j], fidx(j, f, tv), 0)),
            ],
            out_specs=pl.BlockSpec(memory_space=pl.ANY),
            scratch_shapes=[pltpu.VMEM((2, MOE_T, D), F32), pltpu.VMEM((MOE_T, D), BF16),
                            pltpu.VMEM((2, MOE_T, D), F32),
                            pltpu.SemaphoreType.DMA((2,)), pltpu.SemaphoreType.DMA((2,))]),
        compiler_params=_cparams(2),
        name="moe_experts",
    )(tile_e, tile_v, rowmap, h, w13, w13, w2)


def _rowmap_kernel(pos_ref, spare_ref, map_ref, sem):
    fill = pltpu.make_async_copy(spare_ref, map_ref, sem)
    fill.start()
    fill.wait()

    def token(t, c):
        map_ref[pos_ref[t]] = t
        map_ref[pos_ref[NTOK + t]] = NTOK + t
        return c

    lax.fori_loop(0, NTOK, token, 0, unroll=16)


def _rowmap(pos, spare):
    smem = pl.BlockSpec(memory_space=pltpu.SMEM)
    return pl.pallas_call(
        _rowmap_kernel,
        out_shape=jax.ShapeDtypeStruct((MAP_ROWS,), jnp.int32),
        in_specs=[smem, pl.BlockSpec(memory_space=pl.ANY)],
        out_specs=smem,
        scratch_shapes=[pltpu.SemaphoreType.DMA],
        name="moe_rowmap",
    )(pos, spare)


def _combine_kernel(y1_ref, y2_ref, meta_ref, x_ref, mod_ref, fg_ref, yp_ref, ys_ref):
    i = pl.program_id(0)
    meta = meta_ref[...]
    y = meta[:, 4:5] * y1_ref[...] + meta[:, 5:6] * y2_ref[...]
    x = x_ref[...] + mod_ref[5:6] * y
    out = x * lax.rsqrt(jnp.mean(x * x, axis=-1, keepdims=True) + EPS) * fg_ref[...]

    @pl.when(i < ROW_PT)
    def _():
        yp_ref[...] = out

    @pl.when(i >= ROW_PT)
    def _():
        ys_ref[...] = out


def _combine(y, meta, x, mods, final_g):
    slot2 = NTOK // ROW_T
    return pl.pallas_call(
        _combine_kernel,
        out_shape=(jax.ShapeDtypeStruct((NP_TOK, D), F32),
                   jax.ShapeDtypeStruct((NS_TOK, D), F32)),
        grid=(NTOK // ROW_T,),
        in_specs=[pl.BlockSpec((ROW_T, D), lambda i: (i, 0)),
                  pl.BlockSpec((ROW_T, D), lambda i: (slot2 + i, 0)),
                  pl.BlockSpec((ROW_T, LANES), lambda i: (i, 0)),
                  pl.BlockSpec((ROW_T, D), lambda i: (i, 0)),
                  _mod_spec(1, ROW_T),
                  _const_spec((1, D))],
        out_specs=(pl.BlockSpec((ROW_T, D), lambda i: (jnp.minimum(i, ROW_PT - 1), 0)),
                   pl.BlockSpec((ROW_T, D), lambda i: (jnp.maximum(i - ROW_PT, 0), 0))),
        compiler_params=_cparams(1),
        name="moe_combine",
    )(y, y, meta, x, mods, final_g)


def _routing_tables(meta_t, counts):
    cnt = counts[0, :N_EXP].astype(jnp.int32)
    padded = (cnt + MOE_T - 1) // MOE_T * MOE_T
    ends = [padded[0]]
    for e in range(1, N_EXP):
        ends.append(ends[-1] + padded[e])
    ends = jnp.stack(ends)
    off = ends - padded

    def expert_of_row(r):
        return sum((ends[e] <= r).astype(jnp.int32) for e in range(N_EXP - 1))

    def lookup(table, e):
        return sum(jnp.where(e == i, table[i], 0) for i in range(N_EXP))

    e1, e2, r1, r2 = (meta_t[i].astype(jnp.int32) for i in range(4))
    pos = MOE_T + jnp.concatenate([lookup(off, e1) + r1, lookup(off, e2) + r2])

    n_valid = ends[-1] // MOE_T
    tile = jnp.arange(MOE_STEPS, dtype=jnp.int32)
    tile_e = expert_of_row(jnp.minimum(tile, n_valid - 1) * MOE_T)
    tile_v = jnp.concatenate([jnp.zeros((1,), jnp.int32), (tile < n_valid).astype(jnp.int32)])

    row = jnp.arange(MAP_ROWS, dtype=jnp.int32) - MOE_T
    row_e = expert_of_row(row)
    pad_rank = jnp.clip(row - lookup(off + cnt, row_e), 0, MOE_T - 1)
    idle = Y_IDLE + (row & (MOE_T - 1))
    spare = jnp.where((row < 0) | (row >= ends[-1]), idle, 2 * NTOK + row_e * MOE_T + pad_rank)
    return tile_e, tile_v, pos, spare


def kernel(x_prompt, x_sample, cache_k_l1, cache_v_l1, c, c_ctx, ada_w, ada_b, norm_g, conv_w1, conv_b1, conv_dw, conv_dwb, conv_ln_g, conv_ln_b, conv_w2, attn_wqkv, attn_wo, attn_sink, ffn_w13, ffn_w2, moe_router, moe_w13, moe_w2, final_g):
    xp = x_prompt.reshape(NP_TOK, D)
    xs = x_sample.reshape(NS_TOK, D)
    cond = jnp.concatenate([c, c_ctx[None, :], jnp.zeros((N_COND - DEC_BATCH - 1, D), F32)])
    mods = _ada(cond, ada_w, ada_b)

    row = lambda a: a.reshape(1, -1)
    x2, (wqkv_b, wo_b) = _layer0(
        xp, xs, mods, row(norm_g[0, 0]), conv_w1.astype(BF16), row(conv_b1),
        conv_dw, row(conv_dwb), row(conv_ln_g), row(conv_ln_b), conv_w2.astype(BF16),
        row(norm_g[0, 1]), ffn_w13.astype(BF16), ffn_w2.astype(BF16), [attn_wqkv, attn_wo])

    cos, sin = _rope_tables()
    q, k, v, state_k, state_v = _qkv_layer(x2, mods, row(norm_g[1, 0]), wqkv_b, cos, sin)
    o_ctx, o_lat, moe_w2_b, moe_w13_b = _attention(
        q, k, v, cache_k_l1.reshape(DEC_BATCH, PAST, KV_COLS),
        cache_v_l1.reshape(DEC_BATCH, PAST, KV_COLS), attn_sink,
        moe_w2.reshape(N_EXP * D_FFE, D), moe_w13.reshape(N_EXP * D, 2 * D_FFE))
    moe_w13_b = moe_w13_b.reshape(N_EXP, D, 2 * D_FFE)
    moe_w2_b = moe_w2_b.reshape(N_EXP, D_FFE, D)

    x3, h, meta, meta_t, counts = _wo_router(o_ctx, o_lat, x2, mods, row(norm_g[1, 1]),
                                             wo_b, moe_router.T)
    tile_e, tile_v, pos, spare = _routing_tables(meta_t, counts)
    rowmap = _rowmap(pos, spare)
    y = _experts(tile_e, tile_v, rowmap, h, moe_w13_b, moe_w2_b)
    y_p, y_s = _combine(y, meta, x3, mods, row(final_g))

    return (y_p.reshape(BATCH, SEQ, D), y_s.reshape(DEC_BATCH, DEC_SEQ, D),
            state_k.reshape(BATCH, SEQ, N_KV, HD), state_v.reshape(BATCH, SEQ, N_KV, HD))
```
